```python
import functools
import jax, jax.numpy as jnp
from jax import lax
import numpy as np

D_MODEL = 1024
BATCH = 8
SEQ = 4096
DEPTH = 2
DEC_BATCH = 2
DEC_SEQ = 16384
PAST_LEN = 128

SGU_CHUNK = 128
SGU_GROUPS = 8
SGU_WIDTH = D_MODEL
SGU_GROUP_DIM = SGU_WIDTH // SGU_GROUPS
DN_HEADS = 8
DN_HEAD_DIM = D_MODEL // DN_HEADS
DN_WIDTH = DN_HEADS * DN_HEAD_DIM
DN_CHUNK = 64
CONV_K = 5
N_DIR = 2
FFN_DIM = 2816
N_EXPERTS = 8
TOP_K = 2
EXPERT_DIM = 3584
N_DENSE = (DEPTH + 1) // 2
N_MOE = DEPTH // 2
EPS = 1e-6
SPLIT_SIZES = (SGU_WIDTH, SGU_WIDTH, DN_WIDTH, DN_WIDTH, DN_WIDTH, DN_WIDTH,
               N_DIR * DN_HEADS, N_DIR * DN_HEADS, D_MODEL, D_MODEL)
P_TOTAL = sum(SPLIT_SIZES)
SPLIT_POINTS = tuple(int(s) for s in np.cumsum(SPLIT_SIZES)[:-1])

kernel_name = "hybrid_sgu_gdn_encoder"


def rms_norm(x, w):
    xf = x.astype(jnp.float32)
    y = xf * lax.rsqrt(jnp.mean(xf * xf, axis=-1, keepdims=True) + EPS)
    return (y * w.astype(jnp.float32)).astype(x.dtype)


def layer_norm(x, g, b):
    xf = x.astype(jnp.float32)
    mu = jnp.mean(xf, axis=-1, keepdims=True)
    xc = xf - mu
    y = xc * lax.rsqrt(jnp.mean(xc * xc, axis=-1, keepdims=True) + EPS)
    return (y * g.astype(jnp.float32) + b.astype(jnp.float32)).astype(x.dtype)


def l2_normalize(x):
    return x * lax.rsqrt(jnp.sum(x * x, axis=-1, keepdims=True) + EPS)


def spatial_gating(u, v, ln_g, ln_b, w_s, b_s):
    B, L, _ = v.shape
    v = layer_norm(v, ln_g, ln_b)
    vc = v.reshape(B, L // SGU_CHUNK, SGU_CHUNK, SGU_GROUPS, SGU_GROUP_DIM)
    s = jnp.einsum('gpq,bnqgc->bnpgc', w_s, vc) + b_s.T[:, :, None]
    return u * s.reshape(B, L, SGU_WIDTH)


def centred_short_conv(x, w):
    y = lax.conv_general_dilated(
        x, w[:, None, :], window_strides=(1,),
        padding=[(CONV_K // 2, CONV_K // 2)],
        dimension_numbers=('NWC', 'WIO', 'NWC'),
        feature_group_count=x.shape[-1])
    return jax.nn.silu(y)


def chunk_gated_delta_rule(q, k, v, g, beta):
    B, H, L, DK = q.shape
    DV = v.shape[-1]
    C = DN_CHUNK
    N = L // C
    q = q.reshape(B, H, N, C, DK) * (DK ** -0.5)
    k = k.reshape(B, H, N, C, DK)
    v = v.reshape(B, H, N, C, DV)
    beta = beta.reshape(B, H, N, C, 1)
    G = jnp.cumsum(g.reshape(B, H, N, C), axis=-1)
    incl = jnp.tril(jnp.ones((C, C), dtype=bool))
    strict = jnp.tril(jnp.ones((C, C), dtype=bool), -1)
    decay = jnp.exp(jnp.where(incl, G[..., :, None] - G[..., None, :], -jnp.inf))
    kb = k * beta
    a_mat = jnp.where(strict, jnp.einsum('bhnid,bhnjd->bhnij', kb, k) * decay, 0.0)
    rhs = jnp.concatenate([kb * jnp.exp(G)[..., None], v * beta], axis=-1)
    sol = lax.linalg.triangular_solve(a_mat + jnp.eye(C, dtype=jnp.float32), rhs,
                                      left_side=True, lower=True, unit_diagonal=True)
    w, u = sol[..., :DK], sol[..., DK:]
    qk = jnp.einsum('bhnid,bhnjd->bhnij', q, k) * decay
    G_last = G[..., -1:]
    k_to_end = k * jnp.exp(G_last - G)[..., None]
    xs = tuple(jnp.moveaxis(t, 2, 0) for t in
               (q * jnp.exp(G)[..., None], qk, w, u, k_to_end, jnp.exp(G_last)))

    def step(S, xs_c):
        q_c, qk_c, w_c, u_c, k_c, d_c = xs_c
        v_new = u_c - jnp.einsum('bhcd,bhde->bhce', w_c, S)
        o_c = jnp.einsum('bhcd,bhde->bhce', q_c, S) + jnp.einsum('bhij,bhje->bhie', qk_c, v_new)
        S = S * d_c[..., None] + jnp.einsum('bhcd,bhce->bhde', k_c, v_new)
        return S, o_c

    S0 = jnp.zeros((B, H, DK, DV), jnp.float32)
    _, o = lax.scan(step, S0, xs)
    return jnp.moveaxis(o, 0, 2).reshape(B, H, L, DV)


def bidir_gated_deltanet(q, k, v, z, a, b, conv_w, a_log, dt_bias, o_norm):
    B, L, _ = q.shape
    out_dtype = q.dtype
    qkv = centred_short_conv(jnp.concatenate([q, k, v], axis=-1), conv_w)
    q, k, v = jnp.split(qkv, 3, axis=-1)

    def heads(t):
        return t.reshape(B, L, DN_HEADS, DN_HEAD_DIM).astype(jnp.float32).transpose(0, 2, 1, 3)

    q, k, v = l2_normalize(heads(q)), l2_normalize(heads(k)), heads(v)
    a = a.reshape(B, L, N_DIR, DN_HEADS).astype(jnp.float32)
    b = b.reshape(B, L, N_DIR, DN_HEADS).astype(jnp.float32)
    g = -jnp.exp(a_log.astype(jnp.float32)) * jax.nn.softplus(a + dt_bias.astype(jnp.float32))
    beta = jax.nn.sigmoid(b)
    g = g.transpose(2, 0, 3, 1)
    beta = beta.transpose(2, 0, 3, 1)
    o_fwd = chunk_gated_delta_rule(q, k, v, g[0], beta[0])
    flip = lambda t: jnp.flip(t, axis=2)
    o_bwd = flip(chunk_gated_delta_rule(flip(q), flip(k), flip(v), flip(g[1]), flip(beta[1])))
    o = (o_fwd + o_bwd).transpose(0, 2, 1, 3)
    o = o * lax.rsqrt(jnp.mean(o * o, axis=-1, keepdims=True) + EPS) * o_norm.astype(jnp.float32)
    o = o * jax.nn.silu(z.reshape(B, L, DN_HEADS, DN_HEAD_DIM).astype(jnp.float32))
    return o.reshape(B, L, DN_WIDTH).astype(out_dtype)


def swiglu(h, w_gate, w_up, w_down):
    return (jax.nn.silu(h @ w_gate) * (h @ w_up)) @ w_down


def moe_swiglu(h, w_router, w_gate, w_up, w_down):
    shp = h.shape
    t = h.reshape(-1, shp[-1])
    logits = (t @ w_router).astype(jnp.float32)
    top_v, top_i = lax.top_k(logits, TOP_K)
    top_w = jax.nn.softmax(top_v, axis=-1)
    gates = jnp.sum(jax.nn.one_hot(top_i, N_EXPERTS, dtype=jnp.float32) * top_w[..., None], axis=1)
    gates = gates.astype(t.dtype)
    out = jnp.zeros_like(t)
    for e in range(N_EXPERTS):
        out = out + gates[:, e:e + 1] * swiglu(t, w_gate[e], w_up[e], w_down[e])
    return out.reshape(shp)


def encoder_trunk(x, norm_mix, w_in, sgu_ln_gain, sgu_ln_bias, sgu_w_spatial, sgu_b_spatial,
                  dn_conv_w, dn_a_log, dn_dt_bias, dn_out_norm, w_out, norm_ffn,
                  ffn_w_gate, ffn_w_up, ffn_w_down, moe_w_router, moe_w_gate, moe_w_up,
                  moe_w_down, norm_final):
    for i in range(DEPTH):
        h = rms_norm(x, norm_mix[i])
        proj = h @ w_in[i]
        a_u, a_v, b_q, b_k, b_v, b_z, b_a, b_b, g_a, g_b = jnp.split(proj, SPLIT_POINTS, axis=-1)
        out_a = spatial_gating(jax.nn.gelu(a_u, approximate=False), jax.nn.gelu(a_v, approximate=False),
                               sgu_ln_gain[i], sgu_ln_bias[i], sgu_w_spatial[i], sgu_b_spatial[i])
        out_b = bidir_gated_deltanet(b_q, b_k, b_v, b_z, b_a, b_b, dn_conv_w[i],
                                     dn_a_log[i], dn_dt_bias[i], dn_out_norm[i])
        mixed = jax.nn.sigmoid(g_a) * out_a + jax.nn.sigmoid(g_b) * out_b
        x = x + mixed @ w_out[i]
        h = rms_norm(x, norm_ffn[i])
        j = i // 2
        if i % 2 == 0:
            x = x + swiglu(h, ffn_w_gate[j], ffn_w_up[j], ffn_w_down[j])
        else:
            x = x + moe_swiglu(h, moe_w_router[j], moe_w_gate[j], moe_w_up[j], moe_w_down[j])
    return rms_norm(x, norm_final)


def setup_inputs(seed: int = 0) -> dict:
    key = jax.random.key(seed)
    ks = jax.random.split(key, 24)
    f32 = jnp.float32
    nrm = lambda k, shp, s: jax.random.normal(k, shp, f32) * s
    gain = lambda k, shp: 1.0 + 0.02 * jax.random.normal(k, shp, f32)
    dt = jnp.exp(jax.random.uniform(ks[10], (DEPTH, N_DIR, DN_HEADS), f32, np.log(1e-3), np.log(1e-1)))
    return {
        'x_prompt': nrm(ks[0], (BATCH, SEQ, D_MODEL), 1.0),
        'x_sample': nrm(ks[1], (DEC_BATCH, DEC_SEQ, D_MODEL), 1.0),
        'norm_mix': gain(ks[2], (DEPTH, D_MODEL)),
        'w_in': nrm(ks[3], (DEPTH, D_MODEL, P_TOTAL), D_MODEL ** -0.5),
        'sgu_ln_gain': gain(ks[4], (DEPTH, SGU_WIDTH)),
        'sgu_ln_bias': nrm(ks[5], (DEPTH, SGU_WIDTH), 0.02),
        'sgu_w_spatial': nrm(ks[6], (DEPTH, SGU_GROUPS, SGU_CHUNK, SGU_CHUNK), SGU_CHUNK ** -0.5),
        'sgu_b_spatial': gain(ks[7], (DEPTH, SGU_GROUPS, SGU_CHUNK)),
        'dn_conv_w': nrm(ks[8], (DEPTH, CONV_K, 3 * DN_WIDTH), CONV_K ** -0.5),
        'dn_a_log': jnp.log(jax.random.uniform(ks[9], (DEPTH, N_DIR, DN_HEADS), f32, 1.0, 16.0)),
        'dn_dt_bias': dt + jnp.log(-jnp.expm1(-dt)),
        'dn_out_norm': gain(ks[11], (DEPTH, DN_HEAD_DIM)),
        'w_out': nrm(ks[12], (DEPTH, D_MODEL, D_MODEL), D_MODEL ** -0.5),
        'norm_ffn': gain(ks[13], (DEPTH, D_MODEL)),
        'ffn_w_gate': nrm(ks[14], (N_DENSE, D_MODEL, FFN_DIM), D_MODEL ** -0.5),
        'ffn_w_up': nrm(ks[15], (N_DENSE, D_MODEL, FFN_DIM), D_MODEL ** -0.5),
        'ffn_w_down': nrm(ks[16], (N_DENSE, FFN_DIM, D_MODEL), FFN_DIM ** -0.5),
        'moe_w_router': nrm(ks[17], (N_MOE, D_MODEL, N_EXPERTS), D_MODEL ** -0.5),
        'moe_w_gate': nrm(ks[18], (N_MOE, N_EXPERTS, D_MODEL, EXPERT_DIM), D_MODEL ** -0.5),
        'moe_w_up': nrm(ks[19], (N_MOE, N_EXPERTS, D_MODEL, EXPERT_DIM), D_MODEL ** -0.5),
        'moe_w_down': nrm(ks[20], (N_MOE, N_EXPERTS, EXPERT_DIM, D_MODEL), EXPERT_DIM ** -0.5),
        'norm_final': gain(ks[21], (D_MODEL,)),
    }


def reference(x_prompt, x_sample, norm_mix, w_in, sgu_ln_gain, sgu_ln_bias, sgu_w_spatial,
              sgu_b_spatial, dn_conv_w, dn_a_log, dn_dt_bias, dn_out_norm, w_out, norm_ffn,
              ffn_w_gate, ffn_w_up, ffn_w_down, moe_w_router, moe_w_gate, moe_w_up, moe_w_down,
              norm_final):
    run = functools.partial(
        encoder_trunk, norm_mix=norm_mix, w_in=w_in, sgu_ln_gain=sgu_ln_gain,
        sgu_ln_bias=sgu_ln_bias, sgu_w_spatial=sgu_w_spatial, sgu_b_spatial=sgu_b_spatial,
        dn_conv_w=dn_conv_w, dn_a_log=dn_a_log, dn_dt_bias=dn_dt_bias, dn_out_norm=dn_out_norm,
        w_out=w_out, norm_ffn=norm_ffn, ffn_w_gate=ffn_w_gate, ffn_w_up=ffn_w_up,
        ffn_w_down=ffn_w_down, moe_w_router=moe_w_router, moe_w_gate=moe_w_gate,
        moe_w_up=moe_w_up, moe_w_down=moe_w_down, norm_final=norm_final)
    y_prompt = run(x_prompt)
    y_sample = run(x_sample)
    return (y_prompt, y_sample)
```

```python
import functools

import jax
import jax.numpy as jnp
from jax import lax
from jax.experimental import pallas as pl
from jax.experimental.pallas import tpu as pltpu

F32 = jnp.float32
BF16 = jnp.bfloat16
EPS = 1e-6

D_MODEL = 1024
N_HEADS = 8
HEAD_DIM = 128
N_DIR = 2
CONV_K = 5
CONV_HALO = 8
HALO_BLOCK = 16
SGU_CHUNK = 128
SGU_GROUPS = 8
N_EXPERTS = 8
SCAN_CHUNK = 128
MOE_ROW_BLOCK = 256
NEG_BIG = -1e30


def _params(sem, vmem_mb):
    return pltpu.CompilerParams(dimension_semantics=sem, vmem_limit_bytes=vmem_mb * 2**20)


def _dot(a, b):
    return jnp.dot(a.astype(BF16), b.astype(BF16), preferred_element_type=F32)


def _rms(xf, w):
    return xf * lax.rsqrt(jnp.mean(xf * xf, axis=-1, keepdims=True) + EPS) * w


def _silu(x):
    return x * jax.nn.sigmoid(x)


def _gelu(x):
    return 0.5 * x * (1.0 + lax.erf(x * (2.0 ** -0.5)))


def _in_proj_kernel(x_ref, nw_ref, w_ref, wab_ref, p_ref, ab_ref, h_ref):
    @pl.when(pl.program_id(1) == 0)
    def _():
        hb = _rms(x_ref[...], nw_ref[...]).astype(BF16)
        h_ref[...] = hb
        ab_ref[...] = jnp.dot(hb, wab_ref[...], preferred_element_type=F32)

    p_ref[...] = jnp.dot(h_ref[...], w_ref[...], preferred_element_type=F32).astype(BF16)


def _in_proj(x, nw, w_main, w_ab):
    T = x.shape[0]
    tm = min(1024, T)
    tn = 1024
    n_cols = w_main.shape[1]
    return pl.pallas_call(
        _in_proj_kernel,
        grid=(T // tm, n_cols // tn),
        in_specs=[
            pl.BlockSpec((tm, D_MODEL), lambda i, j: (i, 0)),
            pl.BlockSpec((1, D_MODEL), lambda i, j: (0, 0)),
            pl.BlockSpec((D_MODEL, tn), lambda i, j: (0, j)),
            pl.BlockSpec((D_MODEL, 128), lambda i, j: (0, 0)),
        ],
        out_specs=[
            pl.BlockSpec((tm, tn), lambda i, j: (i, j)),
            pl.BlockSpec((tm, 128), lambda i, j: (i, 0)),
        ],
        out_shape=[
            jax.ShapeDtypeStruct((T, n_cols), BF16),
            jax.ShapeDtypeStruct((T, 128), F32),
        ],
        scratch_shapes=[pltpu.VMEM((tm, D_MODEL), BF16)],
        compiler_params=_params(("parallel", "arbitrary"), 48),
        name="in_proj",
    )(x, nw, w_main, w_ab)


def _dn_prep_kernel(tiles_per_seq, pm_ref, pp_ref, pn_ref, ab_ref, cw_ref, nea_ref, dtb_ref,
                    qn_ref, kn_ref, vc_ref, knT_ref, gcol_ref, gT_ref, xp_ref):
    tl = pm_ref.shape[0]
    pos = pl.program_id(0) % tiles_per_seq
    keep_prev = jnp.where(pos == 0, 0.0, 1.0)
    keep_next = jnp.where(pos == tiles_per_seq - 1, 0.0, 1.0)
    q_scale = HEAD_DIM ** -0.5

    for s in range(3):
        cs = slice(s * D_MODEL, (s + 1) * D_MODEL)
        prev = pp_ref[:, cs].astype(F32)[HALO_BLOCK - CONV_HALO:, :] * keep_prev
        nxt = pn_ref[:, cs].astype(F32)[:CONV_HALO, :] * keep_next
        xp_ref[0:CONV_HALO, :] = prev
        xp_ref[CONV_HALO:CONV_HALO + tl, :] = pm_ref[:, cs].astype(F32)
        xp_ref[CONV_HALO + tl:2 * CONV_HALO + tl, :] = nxt
        acc = None
        for k in range(CONV_K):
            start = CONV_HALO - CONV_K // 2 + k
            term = xp_ref[start:start + tl, :] * cw_ref[k:k + 1, cs]
            acc = term if acc is None else acc + term
        y = _silu(acc)
        if s == 2:
            vc_ref[...] = y.astype(BF16)
            continue
        for h in range(N_HEADS):
            hs = slice(h * HEAD_DIM, (h + 1) * HEAD_DIM)
            slab = y[:, hs]
            slab = slab * lax.rsqrt(jnp.sum(slab * slab, axis=-1, keepdims=True) + EPS)
            if s == 0:
                qn_ref[:, hs] = (slab * q_scale).astype(BF16)
            else:
                kn_ref[:, hs] = slab.astype(BF16)
                knT_ref[hs, :] = jnp.transpose(slab).astype(BF16)

    ab = ab_ref[...]
    z = ab + dtb_ref[...]
    softplus = jnp.maximum(z, 0.0) + jnp.log(1.0 + jnp.exp(-jnp.abs(z)))
    g = nea_ref[...] * softplus
    beta = jax.nn.sigmoid(ab)
    ri = lax.broadcasted_iota(jnp.int32, (tl, tl), 0)
    ci = lax.broadcasted_iota(jnp.int32, (tl, tl), 1)
    same = jnp.right_shift(ri, 7) == jnp.right_shift(ci, 7)
    lower = jnp.where(same & (ci <= ri), 1.0, 0.0).astype(BF16)
    upper = jnp.where(same & (ci >= ri), 1.0, 0.0).astype(BF16)
    g_hi = g.astype(BF16)
    g_lo = (g - g_hi.astype(F32)).astype(BF16)
    g_fwd = (jnp.dot(lower, g_hi, preferred_element_type=F32)
             + jnp.dot(lower, g_lo, preferred_element_type=F32))
    g_bwd = (jnp.dot(upper, g_hi, preferred_element_type=F32)
             + jnp.dot(upper, g_lo, preferred_element_type=F32))
    lane = lax.broadcasted_iota(jnp.int32, g.shape, 1)
    gcol = jnp.where(lane < N_HEADS, g_fwd, jnp.where(lane < N_DIR * N_HEADS, g_bwd, beta))
    gcol_ref[...] = gcol
    gT_ref[...] = jnp.transpose(gcol)[0:N_DIR * N_HEADS, :]


def _dn_prep(P, ab, conv_w, neg_exp_alog, dt_bias, seq_len):
    T = P.shape[0]
    tl = min(256, seq_len)
    tiles_per_seq = seq_len // tl
    hb = tl // HALO_BLOCK
    n_hb = T // HALO_BLOCK
    qkv_w = 3 * D_MODEL
    tok = jax.ShapeDtypeStruct((T, D_MODEL), BF16)
    return pl.pallas_call(
        functools.partial(_dn_prep_kernel, tiles_per_seq),
        grid=(T // tl,),
        in_specs=[
            pl.BlockSpec((tl, qkv_w), lambda i: (i, 0)),
            pl.BlockSpec((HALO_BLOCK, qkv_w), lambda i: (jnp.maximum(i * hb - 1, 0), 0)),
            pl.BlockSpec((HALO_BLOCK, qkv_w), lambda i: (jnp.minimum((i + 1) * hb, n_hb - 1), 0)),
            pl.BlockSpec((tl, 128), lambda i: (i, 0)),
            pl.BlockSpec((CONV_K, qkv_w), lambda i: (0, 0)),
            pl.BlockSpec((1, 128), lambda i: (0, 0)),
            pl.BlockSpec((1, 128), lambda i: (0, 0)),
        ],
        out_specs=[
            pl.BlockSpec((tl, D_MODEL), lambda i: (i, 0)),
            pl.BlockSpec((tl, D_MODEL), lambda i: (i, 0)),
            pl.BlockSpec((tl, D_MODEL), lambda i: (i, 0)),
            pl.BlockSpec((D_MODEL, tl), lambda i: (0, i)),
            pl.BlockSpec((tl, 128), lambda i: (i, 0)),
            pl.BlockSpec((N_DIR * N_HEADS, tl), lambda i: (0, i)),
        ],
        out_shape=[tok, tok, tok,
                   jax.ShapeDtypeStruct((D_MODEL, T), BF16),
                   jax.ShapeDtypeStruct((T, 128), F32),
                   jax.ShapeDtypeStruct((N_DIR * N_HEADS, T), F32)],
        scratch_shapes=[pltpu.VMEM((tl + 2 * CONV_HALO, D_MODEL), F32)],
        compiler_params=_params(("parallel",), 48),
        name="dn_prep",
    )(P, P, P, ab, conv_w, neg_exp_alog, dt_bias)


def _unit_tri_inverse(A, ri, ci):
    eye = jnp.where(ri == ci, 1.0, 0.0)
    blk = lambda sh: jnp.right_shift(ri, sh) == jnp.right_shift(ci, sh)
    m16, m32, m64 = blk(4), blk(5), blk(6)
    Dg = jnp.where(m16, A, 0.0)
    D2 = _dot(Dg, Dg)
    D4 = _dot(D2, D2)
    D8 = _dot(D4, D4)
    X = eye - Dg
    X = X + _dot(X, D2)
    X = X + _dot(X, D4)
    X = X + _dot(X, D8)
    for inner, outer in ((m16, m32), (m32, m64), (m64, None)):
        off = ~inner if outer is None else (outer & ~inner)
        O = jnp.where(off, A, 0.0)
        X = X - _dot(_dot(X, O), X)
    return X


def _dn_scan_kernel(direction, q_ref, k_ref, v_ref, kT_ref, gc_ref, gT_ref, o_ref, S_ref):
    CH = SCAN_CHUNK

    @pl.when(pl.program_id(1) == 0)
    def _():
        S_ref[...] = jnp.zeros_like(S_ref)

    ri = lax.broadcasted_iota(jnp.int32, (CH, CH), 0)
    ci = lax.broadcasted_iota(jnp.int32, (CH, CH), 1)
    if direction == 0:
        incl, strict, end = ri >= ci, ri > ci, CH - 1
    else:
        incl, strict, end = ri <= ci, ri < ci, 0
    gc = gc_ref[...]
    gT = gT_ref[...]
    for h in range(N_HEADS):
        hs = slice(h * HEAD_DIM, (h + 1) * HEAD_DIM)
        lane = direction * N_HEADS + h
        Gc = gc[:, lane:lane + 1]
        beta = gc[:, N_DIR * N_HEADS + lane:N_DIR * N_HEADS + lane + 1]
        Gr = gT[lane:lane + 1, :]
        G_end = Gr[:, end:end + 1]
        q = q_ref[:, hs]
        k = k_ref[:, hs]
        v = v_ref[:, hs].astype(F32)
        kT = kT_ref[hs, :]
        decay = jnp.exp(jnp.where(incl, Gc - Gr, NEG_BIG))
        KK = jnp.dot(k, kT, preferred_element_type=F32)
        QK = jnp.dot(q, kT, preferred_element_type=F32)
        A = jnp.where(strict, beta * KK * decay, 0.0)
        Tm = _unit_tri_inverse(A, ri, ci)
        eG = jnp.exp(Gc)
        kf = k.astype(F32)
        rhs = jnp.concatenate([kf * (beta * eG), v * beta], axis=-1)
        sol = _dot(Tm, rhs)
        w, u = sol[:, :HEAD_DIM], sol[:, HEAD_DIM:]
        qg = q.astype(F32) * eG
        S = S_ref[h]
        R = _dot(jnp.concatenate([w, qg], axis=0), S)
        v_new = u - R[:CH]
        o = R[CH:] + _dot(QK * decay, v_new)
        keT = kT.astype(F32) * jnp.exp(G_end - Gr)
        S_ref[h] = S * jnp.exp(G_end) + _dot(keT, v_new)
        o_ref[:, hs] = o.astype(o_ref.dtype)


def _dn_scan(qn, kn, vc, knT, gcol, gT, seq_len, direction):
    T = qn.shape[0]
    CH = SCAN_CHUNK
    nc = seq_len // CH
    nb = T // seq_len
    if direction == 0:
        blk = lambda b, c: b * nc + c
    else:
        blk = lambda b, c: b * nc + (nc - 1 - c)
    tok_spec = pl.BlockSpec((CH, D_MODEL), lambda b, c: (blk(b, c), 0))
    return pl.pallas_call(
        functools.partial(_dn_scan_kernel, direction),
        grid=(nb, nc),
        in_specs=[
            tok_spec, tok_spec, tok_spec,
            pl.BlockSpec((D_MODEL, CH), lambda b, c: (0, blk(b, c))),
            pl.BlockSpec((CH, 128), lambda b, c: (blk(b, c), 0)),
            pl.BlockSpec((N_DIR * N_HEADS, CH), lambda b, c: (0, blk(b, c))),
        ],
        out_specs=tok_spec,
        out_shape=jax.ShapeDtypeStruct((T, D_MODEL), BF16),
        scratch_shapes=[pltpu.VMEM((N_HEADS, HEAD_DIM, HEAD_DIM), F32)],
        compiler_params=_params(("parallel", "arbitrary"), 32),
        name=f"dn_scan_{direction}",
    )(qn, kn, vc, knT, gcol, gT)


def _mix_kernel(u_ref, vs_ref, z_ref, ga_ref, gb_ref, of_ref, ob_ref, x_ref,
                lng_ref, lnb_ref, ws_ref, bs_ref, on_ref, wo_ref, out_ref, mix_ref):
    tm = x_ref.shape[0]
    v = _gelu(vs_ref[...].astype(F32))
    vcen = v - jnp.mean(v, axis=-1, keepdims=True)
    vln = vcen * lax.rsqrt(jnp.mean(vcen * vcen, axis=-1, keepdims=True) + EPS)
    vb = (vln * lng_ref[...] + lnb_ref[...]).astype(BF16)
    for g in range(SGU_GROUPS):
        gs = slice(g * HEAD_DIM, (g + 1) * HEAD_DIM)
        u = _gelu(u_ref[:, gs].astype(F32))
        gate_a = jax.nn.sigmoid(ga_ref[:, gs].astype(F32))
        o = of_ref[:, gs].astype(F32) + ob_ref[:, gs].astype(F32)
        o = o * lax.rsqrt(jnp.mean(o * o, axis=-1, keepdims=True) + EPS) * on_ref[...]
        o = o * _silu(z_ref[:, gs].astype(F32))
        mixed = jax.nn.sigmoid(gb_ref[:, gs].astype(F32)) * o
        w_sp = ws_ref[g]
        bias = bs_ref[:, g:g + 1]
        for c in range(tm // SGU_CHUNK):
            rs = slice(c * SGU_CHUNK, (c + 1) * SGU_CHUNK)
            s = jnp.dot(w_sp, vb[rs, gs], preferred_element_type=F32) + bias
            mix_ref[rs, gs] = (mixed[rs] + gate_a[rs] * (u[rs] * s)).astype(BF16)
    out_ref[...] = x_ref[...] + jnp.dot(mix_ref[...], wo_ref[...], preferred_element_type=F32)


def _mix(P, o_f, o_b, x, ln_g, ln_b, w_sp, b_spT, o_norm, w_out):
    T = x.shape[0]
    tm = min(256, T)
    col = lambda j: pl.BlockSpec((tm, D_MODEL), lambda i: (i, j))
    tok = pl.BlockSpec((tm, D_MODEL), lambda i: (i, 0))
    vec = pl.BlockSpec((1, D_MODEL), lambda i: (0, 0))
    return pl.pallas_call(
        _mix_kernel,
        grid=(T // tm,),
        in_specs=[
            col(3), col(4), col(5), col(6), col(7), tok, tok, tok, vec, vec,
            pl.BlockSpec((SGU_GROUPS, SGU_CHUNK, SGU_CHUNK), lambda i: (0, 0, 0)),
            pl.BlockSpec((SGU_CHUNK, SGU_GROUPS), lambda i: (0, 0)),
            pl.BlockSpec((1, HEAD_DIM), lambda i: (0, 0)),
            pl.BlockSpec((D_MODEL, D_MODEL), lambda i: (0, 0)),
        ],
        out_specs=tok,
        out_shape=jax.ShapeDtypeStruct((T, D_MODEL), F32),
        scratch_shapes=[pltpu.VMEM((tm, D_MODEL), BF16)],
        compiler_params=_params(("parallel",), 48),
        name="mix",
    )(P, P, P, P, P, o_f, o_b, x, ln_g, ln_b, w_sp, b_spT, o_norm, w_out)


def _ffn_kernel(x_ref, nw_ref, wg_ref, wu_ref, wd_ref, o_ref, h_ref):
    @pl.when(pl.program_id(1) == 0)
    def _():
        xf = x_ref[...]
        h_ref[...] = _rms(xf, nw_ref[...]).astype(BF16)
        o_ref[...] = xf

    h = h_ref[...]
    g = jnp.dot(h, wg_ref[...], preferred_element_type=F32)
    u = jnp.dot(h, wu_ref[...], preferred_element_type=F32)
    o_ref[...] += jnp.dot((_silu(g) * u).astype(BF16), wd_ref[...], preferred_element_type=F32)


def _ffn(x, nw, wg, wu, wd):
    T = x.shape[0]
    F = wg.shape[1]
    tm = min(1024, T)
    tf = 256
    tok = pl.BlockSpec((tm, D_MODEL), lambda i, f: (i, 0))
    return pl.pallas_call(
        _ffn_kernel,
        grid=(T // tm, F // tf),
        in_specs=[
            tok,
            pl.BlockSpec((1, D_MODEL), lambda i, f: (0, 0)),
            pl.BlockSpec((D_MODEL, tf), lambda i, f: (0, f)),
            pl.BlockSpec((D_MODEL, tf), lambda i, f: (0, f)),
            pl.BlockSpec((tf, D_MODEL), lambda i, f: (f, 0)),
        ],
        out_specs=tok,
        out_shape=jax.ShapeDtypeStruct((T, D_MODEL), F32),
        scratch_shapes=[pltpu.VMEM((tm, D_MODEL), BF16)],
        compiler_params=_params(("parallel", "arbitrary"), 48),
        name="ffn",
    )(x, nw, wg, wu, wd)


def _moe_kernel(nf, x_ref, nw_ref, wrh_ref, wrl_ref, wg_ref, wu_ref, wd_ref, nfin_ref, o_ref,
                h_ref, rank_ref, gate_ref, cnt_ref, xe_ref, ye_ref):
    e = pl.program_id(1)
    f = pl.program_id(2)
    Tm = x_ref.shape[0]
    RB = min(MOE_ROW_BLOCK, Tm)
    nt = (((1,), (1,)), ((), ()))
    tn = (((0,), (0,)), ((), ()))

    @pl.when((e == 0) & (f == 0))
    def _route():
        xf = x_ref[...]
        h = _rms(xf, nw_ref[...])
        hb = h.astype(BF16)
        h_ref[...] = hb
        hl = (h - hb.astype(F32)).astype(BF16)
        wrh = wrh_ref[...]
        logits = (lax.dot_general(wrh, hb, nt, preferred_element_type=F32)
                  + lax.dot_general(wrh, hl, nt, preferred_element_type=F32)
                  + lax.dot_general(wrl_ref[...], hb, nt, preferred_element_type=F32))
        ei = lax.broadcasted_iota(jnp.int32, logits.shape, 0)
        m1 = jnp.max(logits, axis=0, keepdims=True)
        i1 = jnp.min(jnp.where(logits == m1, ei, N_EXPERTS), axis=0, keepdims=True)
        s1 = ei == i1
        rest = jnp.where(s1, -jnp.inf, logits)
        m2 = jnp.max(rest, axis=0, keepdims=True)
        i2 = jnp.min(jnp.where(rest == m2, ei, N_EXPERTS), axis=0, keepdims=True)
        s2 = ei == i2
        e2 = jnp.exp(m2 - m1)
        w1 = 1.0 / (1.0 + e2)
        gate_ref[...] = jnp.where(s1, w1, 0.0) + jnp.where(s2, e2 * w1, 0.0)
        sel = jnp.where(s1 | s2, 1.0, 0.0)
        ri = lax.broadcasted_iota(jnp.int32, (Tm, Tm), 0)
        ci = lax.broadcasted_iota(jnp.int32, (Tm, Tm), 1)
        before = jnp.where(ri < ci, 1.0, 0.0).astype(BF16)
        rank = jnp.dot(sel.astype(BF16), before, preferred_element_type=F32)
        rank_ref[...] = jnp.where(sel > 0.0, rank, -1.0)
        cnt = jnp.sum(sel, axis=1, keepdims=True)
        for ee in range(N_EXPERTS):
            cnt_ref[ee] = cnt[ee, 0].astype(jnp.int32)
        o_ref[...] = xf

    n_blocks = (cnt_ref[e] + RB - 1) // RB
    rk = rank_ref[pl.ds(e, 1), :]

    def one_hot(rb, value):
        rows = (lax.broadcasted_iota(jnp.int32, (RB, Tm), 0) + rb * RB).astype(F32)
        return jnp.where(rk == rows, value, 0.0).astype(BF16)

    @pl.when(f == 0)
    def _gather():
        def body(rb, carry):
            rows = pl.ds(pl.multiple_of(rb * RB, RB), RB)
            xe_ref[rows, :] = jnp.dot(one_hot(rb, 1.0), h_ref[...],
                                      preferred_element_type=F32).astype(BF16)
            ye_ref[rows, :] = jnp.zeros((RB, D_MODEL), F32)
            return carry
        lax.fori_loop(0, n_blocks, body, 0)

    def expert_body(rb, carry):
        rows = pl.ds(pl.multiple_of(rb * RB, RB), RB)
        xb = xe_ref[rows, :]
        g = jnp.dot(xb, wg_ref[0], preferred_element_type=F32)
        u = jnp.dot(xb, wu_ref[0], preferred_element_type=F32)
        ye_ref[rows, :] += jnp.dot((_silu(g) * u).astype(BF16), wd_ref[0],
                                   preferred_element_type=F32)
        return carry
    lax.fori_loop(0, n_blocks, expert_body, 0)

    @pl.when(f == nf - 1)
    def _scatter():
        gt = gate_ref[pl.ds(e, 1), :]

        def body(rb, carry):
            rows = pl.ds(pl.multiple_of(rb * RB, RB), RB)
            o_ref[...] += lax.dot_general(one_hot(rb, gt), ye_ref[rows, :].astype(BF16), tn,
                                          preferred_element_type=F32)
            return carry
        lax.fori_loop(0, n_blocks, body, 0)

    @pl.when((e == N_EXPERTS - 1) & (f == nf - 1))
    def _final():
        o_ref[...] = _rms(o_ref[...], nfin_ref[...])


def _moe(x, nw, wr_hi, wr_lo, wg, wu, wd, n_final):
    T = x.shape[0]
    F = wg.shape[2]
    tm = min(1024, T)
    tf = 896
    nf = F // tf
    tok = pl.BlockSpec((tm, D_MODEL), lambda i, e, f: (i, 0))
    vec = pl.BlockSpec((1, D_MODEL), lambda i, e, f: (0, 0))
    wr = pl.BlockSpec((N_EXPERTS, D_MODEL), lambda i, e, f: (0, 0))
    return pl.pallas_call(
        functools.partial(_moe_kernel, nf),
        grid=(T // tm, N_EXPERTS, nf),
        in_specs=[
            tok, vec, wr, wr,
            pl.BlockSpec((1, D_MODEL, tf), lambda i, e, f: (e, 0, f)),
            pl.BlockSpec((1, D_MODEL, tf), lambda i, e, f: (e, 0, f)),
            pl.BlockSpec((1, tf, D_MODEL), lambda i, e, f: (e, f, 0)),
            vec,
        ],
        out_specs=tok,
        out_shape=jax.ShapeDtypeStruct((T, D_MODEL), F32),
        scratch_shapes=[
            pltpu.VMEM((tm, D_MODEL), BF16),
            pltpu.VMEM((N_EXPERTS, tm), F32),
            pltpu.VMEM((N_EXPERTS, tm), F32),
            pltpu.SMEM((N_EXPERTS,), jnp.int32),
            pltpu.VMEM((tm, D_MODEL), BF16),
            pltpu.VMEM((tm, D_MODEL), F32),
        ],
        compiler_params=_params(("parallel", "arbitrary", "arbitrary"), 56),
        name="moe",
    )(x, nw, wr_hi, wr_lo, wg, wu, wd, n_final)


def _pad_lanes(v, width=128):
    v = v.reshape(1, -1).astype(F32)
    return jnp.pad(v, ((0, 0), (0, width - v.shape[1])))


def _prepare(norm_mix, w_in, sgu_ln_gain, sgu_ln_bias, sgu_w_spatial, sgu_b_spatial,
             dn_conv_w, dn_a_log, dn_dt_bias, dn_out_norm, w_out, norm_ffn,
             ffn_w_gate, ffn_w_up, ffn_w_down, moe_w_router, moe_w_gate, moe_w_up,
             moe_w_down, norm_final):
    depth = w_in.shape[0]
    W = D_MODEL
    layers = []
    for i in range(depth):
        wi = w_in[i]
        seg = lambda a, b: wi[:, a:b]
        ab0 = 6 * W
        ab1 = ab0 + 2 * N_DIR * N_HEADS
        w_main = jnp.concatenate(
            [seg(2 * W, 5 * W), seg(0, 2 * W), seg(5 * W, 6 * W), seg(ab1, ab1 + 2 * W)],
            axis=1).astype(BF16)
        w_ab = jnp.pad(seg(ab0, ab1), ((0, 0), (0, 128 - (ab1 - ab0)))).astype(BF16)
        lp = dict(
            norm_mix=norm_mix[i].reshape(1, W), w_main=w_main, w_ab=w_ab,
            ln_g=sgu_ln_gain[i].reshape(1, W), ln_b=sgu_ln_bias[i].reshape(1, W),
            w_sp=sgu_w_spatial[i].astype(BF16), b_spT=jnp.transpose(sgu_b_spatial[i]),
            conv_w=dn_conv_w[i],
            neg_exp_alog=_pad_lanes(-jnp.exp(dn_a_log[i].astype(F32))),
            dt_bias=_pad_lanes(dn_dt_bias[i]),
            o_norm=dn_out_norm[i].reshape(1, HEAD_DIM), w_out=w_out[i].astype(BF16),
            norm_ffn=norm_ffn[i].reshape(1, W),
        )
        j = i // 2
        if i % 2 == 0:
            lp.update(wg=ffn_w_gate[j].astype(BF16), wu=ffn_w_up[j].astype(BF16),
                      wd=ffn_w_down[j].astype(BF16))
        else:
            wrT = jnp.transpose(moe_w_router[j]).astype(F32)
            wr_hi = wrT.astype(BF16)
            lp.update(wr_hi=wr_hi, wr_lo=(wrT - wr_hi.astype(F32)).astype(BF16),
                      wg=moe_w_gate[j].astype(BF16), wu=moe_w_up[j].astype(BF16),
                      wd=moe_w_down[j].astype(BF16))
        layers.append(lp)
    return layers, norm_final.reshape(1, W)


def _trunk(x3, layers, n_final):
    B, L, W = x3.shape
    x = x3.reshape(B * L, W)
    depth = len(layers)
    assert depth % 2 == 0, "the final RMSNorm is fused into the last (expert) layer"
    for i, lp in enumerate(layers):
        P, ab = _in_proj(x, lp["norm_mix"], lp["w_main"], lp["w_ab"])
        qn, kn, vc, knT, gcol, gT = _dn_prep(P, ab, lp["conv_w"], lp["neg_exp_alog"],
                                             lp["dt_bias"], L)
        o_f = _dn_scan(qn, kn, vc, knT, gcol, gT, L, 0)
        o_b = _dn_scan(qn, kn, vc, knT, gcol, gT, L, 1)
        x = _mix(P, o_f, o_b, x, lp["ln_g"], lp["ln_b"], lp["w_sp"], lp["b_spT"],
                 lp["o_norm"], lp["w_out"])
        if i % 2 == 0:
            x = _ffn(x, lp["norm_ffn"], lp["wg"], lp["wu"], lp["wd"])
        else:
            assert i == depth - 1
            x = _moe(x, lp["norm_ffn"], lp["wr_hi"], lp["wr_lo"], lp["wg"], lp["wu"], lp["wd"],
                     n_final)
    return x.reshape(B, L, W)


def kernel(x_prompt, x_sample, norm_mix, w_in, sgu_ln_gain, sgu_ln_bias, sgu_w_spatial, sgu_b_spatial, dn_conv_w, dn_a_log, dn_dt_bias, dn_out_norm, w_out, norm_ffn, ffn_w_gate, ffn_w_up, ffn_w_down, moe_w_router, moe_w_gate, moe_w_up, moe_w_down, norm_final):
    layers, n_final = _prepare(norm_mix, w_in, sgu_ln_gain, sgu_ln_bias, sgu_w_spatial,
                               sgu_b_spatial, dn_conv_w, dn_a_log, dn_dt_bias, dn_out_norm,
                               w_out, norm_ffn, ffn_w_gate, ffn_w_up, ffn_w_down, moe_w_router,
                               moe_w_gate, moe_w_up, moe_w_down, norm_final)
    return (_trunk(x_prompt, layers, n_final), _trunk(x_sample, layers, n_final))
```

```python
import functools

import jax
import jax.numpy as jnp
from jax import lax
from jax.experimental import pallas as pl
from jax.experimental.pallas import tpu as pltpu

F32 = jnp.float32
BF16 = jnp.bfloat16
EPS = 1e-6

D_MODEL = 1024
N_HEADS = 8
HEAD_DIM = 128
N_DIR = 2
CONV_K = 5
CONV_HALO = 8
HALO_BLOCK = 16
SGU_CHUNK = 128
SGU_GROUPS = 8
N_EXPERTS = 8
SCAN_CHUNK = 128
MOE_ROW_BLOCK = 288
NEG_BIG = -1e30


def _params(sem, vmem_mb):
    return pltpu.CompilerParams(dimension_semantics=sem, vmem_limit_bytes=vmem_mb * 2**20)


def _dot(a, b):
    return jnp.dot(a.astype(BF16), b.astype(BF16), preferred_element_type=F32)


def _rms(xf, w):
    return xf * lax.rsqrt(jnp.mean(xf * xf, axis=-1, keepdims=True) + EPS) * w


def _silu(x):
    return x * jax.nn.sigmoid(x)


def _gelu(x):
    return 0.5 * x * (1.0 + lax.erf(x * (2.0 ** -0.5)))


def _in_proj_kernel(x_ref, nw_ref, w_ref, wab_ref, p_ref, ab_ref, h_ref):
    @pl.when(pl.program_id(1) == 0)
    def _():
        hb = _rms(x_ref[...], nw_ref[...]).astype(BF16)
        h_ref[...] = hb
        ab_ref[...] = jnp.dot(hb, wab_ref[...], preferred_element_type=F32)

    p_ref[...] = jnp.dot(h_ref[...], w_ref[...], preferred_element_type=F32).astype(BF16)


def _in_proj(x, nw, w_main, w_ab):
    T = x.shape[0]
    tm = min(1024, T)
    tn = 1024
    n_cols = w_main.shape[1]
    return pl.pallas_call(
        _in_proj_kernel,
        grid=(T // tm, n_cols // tn),
        in_specs=[
            pl.BlockSpec((tm, D_MODEL), lambda i, j: (i, 0)),
            pl.BlockSpec((1, D_MODEL), lambda i, j: (0, 0)),
            pl.BlockSpec((D_MODEL, tn), lambda i, j: (0, j)),
            pl.BlockSpec((D_MODEL, 128), lambda i, j: (0, 0)),
        ],
        out_specs=[
            pl.BlockSpec((tm, tn), lambda i, j: (i, j)),
            pl.BlockSpec((tm, 128), lambda i, j: (i, 0)),
        ],
        out_shape=[
            jax.ShapeDtypeStruct((T, n_cols), BF16),
            jax.ShapeDtypeStruct((T, 128), F32),
        ],
        scratch_shapes=[pltpu.VMEM((tm, D_MODEL), BF16)],
        compiler_params=_params(("parallel", "arbitrary"), 48),
        name="in_proj",
    )(x, nw, w_main, w_ab)


def _dn_prep_kernel(tiles_per_seq, pm_ref, pp_ref, pn_ref, ab_ref, cw_ref, nea_ref, dtb_ref,
                    qn_ref, kn_ref, vc_ref, knT_ref, gcol_ref, gT_ref, xp_ref):
    tl = pm_ref.shape[0]
    pos = pl.program_id(0) % tiles_per_seq
    keep_prev = jnp.where(pos == 0, 0.0, 1.0)
    keep_next = jnp.where(pos == tiles_per_seq - 1, 0.0, 1.0)
    q_scale = HEAD_DIM ** -0.5

    for s in range(3):
        cs = slice(s * D_MODEL, (s + 1) * D_MODEL)
        prev = pp_ref[:, cs].astype(F32)[HALO_BLOCK - CONV_HALO:, :] * keep_prev
        nxt = pn_ref[:, cs].astype(F32)[:CONV_HALO, :] * keep_next
        xp_ref[0:CONV_HALO, :] = prev
        xp_ref[CONV_HALO:CONV_HALO + tl, :] = pm_ref[:, cs].astype(F32)
        xp_ref[CONV_HALO + tl:2 * CONV_HALO + tl, :] = nxt
        acc = None
        for k in range(CONV_K):
            start = CONV_HALO - CONV_K // 2 + k
            term = xp_ref[start:start + tl, :] * cw_ref[k:k + 1, cs]
            acc = term if acc is None else acc + term
        y = _silu(acc)
        if s == 2:
            vc_ref[...] = y.astype(BF16)
            continue
        for h in range(N_HEADS):
            hs = slice(h * HEAD_DIM, (h + 1) * HEAD_DIM)
            slab = y[:, hs]
            slab = slab * lax.rsqrt(jnp.sum(slab * slab, axis=-1, keepdims=True) + EPS)
            if s == 0:
                qn_ref[:, hs] = (slab * q_scale).astype(BF16)
            else:
                kn_ref[:, hs] = slab.astype(BF16)
                knT_ref[hs, :] = jnp.transpose(slab).astype(BF16)

    ab = ab_ref[...]
    z = ab + dtb_ref[...]
    softplus = jnp.maximum(z, 0.0) + jnp.log(1.0 + jnp.exp(-jnp.abs(z)))
    g = nea_ref[...] * softplus
    beta = jax.nn.sigmoid(ab)
    ri = lax.broadcasted_iota(jnp.int32, (tl, tl), 0)
    ci = lax.broadcasted_iota(jnp.int32, (tl, tl), 1)
    same = jnp.right_shift(ri, 7) == jnp.right_shift(ci, 7)
    lower = jnp.where(same & (ci <= ri), 1.0, 0.0).astype(BF16)
    upper = jnp.where(same & (ci >= ri), 1.0, 0.0).astype(BF16)
    g_hi = g.astype(BF16)
    g_lo = (g - g_hi.astype(F32)).astype(BF16)
    g_fwd = (jnp.dot(lower, g_hi, preferred_element_type=F32)
             + jnp.dot(lower, g_lo, preferred_element_type=F32))
    g_bwd = (jnp.dot(upper, g_hi, preferred_element_type=F32)
             + jnp.dot(upper, g_lo, preferred_element_type=F32))
    lane = lax.broadcasted_iota(jnp.int32, g.shape, 1)
    gcol = jnp.where(lane < N_HEADS, g_fwd, jnp.where(lane < N_DIR * N_HEADS, g_bwd, beta))
    gcol_ref[...] = gcol
    gT_ref[...] = jnp.transpose(gcol)[0:N_DIR * N_HEADS, :]


def _dn_prep(P, ab, conv_w, neg_exp_alog, dt_bias, seq_len):
    T = P.shape[0]
    tl = min(256, seq_len)
    tiles_per_seq = seq_len // tl
    hb = tl // HALO_BLOCK
    n_hb = T // HALO_BLOCK
    qkv_w = 3 * D_MODEL
    tok = jax.ShapeDtypeStruct((T, D_MODEL), BF16)
    return pl.pallas_call(
        functools.partial(_dn_prep_kernel, tiles_per_seq),
        grid=(T // tl,),
        in_specs=[
            pl.BlockSpec((tl, qkv_w), lambda i: (i, 0)),
            pl.BlockSpec((HALO_BLOCK, qkv_w), lambda i: (jnp.maximum(i * hb - 1, 0), 0)),
            pl.BlockSpec((HALO_BLOCK, qkv_w), lambda i: (jnp.minimum((i + 1) * hb, n_hb - 1), 0)),
            pl.BlockSpec((tl, 128), lambda i: (i, 0)),
            pl.BlockSpec((CONV_K, qkv_w), lambda i: (0, 0)),
            pl.BlockSpec((1, 128), lambda i: (0, 0)),
            pl.BlockSpec((1, 128), lambda i: (0, 0)),
        ],
        out_specs=[
            pl.BlockSpec((tl, D_MODEL), lambda i: (i, 0)),
            pl.BlockSpec((tl, D_MODEL), lambda i: (i, 0)),
            pl.BlockSpec((tl, D_MODEL), lambda i: (i, 0)),
            pl.BlockSpec((D_MODEL, tl), lambda i: (0, i)),
            pl.BlockSpec((tl, 128), lambda i: (i, 0)),
            pl.BlockSpec((N_DIR * N_HEADS, tl), lambda i: (0, i)),
        ],
        out_shape=[tok, tok, tok,
                   jax.ShapeDtypeStruct((D_MODEL, T), BF16),
                   jax.ShapeDtypeStruct((T, 128), F32),
                   jax.ShapeDtypeStruct((N_DIR * N_HEADS, T), F32)],
        scratch_shapes=[pltpu.VMEM((tl + 2 * CONV_HALO, D_MODEL), F32)],
        compiler_params=_params(("parallel",), 48),
        name="dn_prep",
    )(P, P, P, ab, conv_w, neg_exp_alog, dt_bias)


def _stack(a, b):
    return jnp.concatenate([a.astype(BF16), b.astype(BF16)], axis=0)


def _dn_scan_kernel(direction, q_ref, k_ref, v_ref, kT_ref, gc_ref, gT_ref, o_ref, S_ref):
    CH = SCAN_CHUNK
    HEADS = range(N_HEADS)

    @pl.when(pl.program_id(1) == 0)
    def _():
        S_ref[...] = jnp.zeros_like(S_ref)

    ri = lax.broadcasted_iota(jnp.int32, (CH, CH), 0)
    ci = lax.broadcasted_iota(jnp.int32, (CH, CH), 1)
    if direction == 0:
        incl, strict, end = ri >= ci, ri > ci, CH - 1
    else:
        incl, strict, end = ri <= ci, ri < ci, 0
    eye = jnp.where(ri == ci, 1.0, 0.0)
    blk = lambda sh: jnp.right_shift(ri, sh) == jnp.right_shift(ci, sh)
    m16, m32, m64 = blk(4), blk(5), blk(6)
    levels = (m32 & ~m16, m64 & ~m32, ~m64)
    hs = [slice(h * HEAD_DIM, (h + 1) * HEAD_DIM) for h in HEADS]
    gc = gc_ref[...]
    gT = gT_ref[...]
    lanes = [direction * N_HEADS + h for h in HEADS]
    Gc = [gc[:, l:l + 1] for l in lanes]
    beta = [gc[:, N_DIR * N_HEADS + l:N_DIR * N_HEADS + l + 1] for l in lanes]
    Gr = [gT[l:l + 1, :] for l in lanes]
    G_end = [g[:, end:end + 1] for g in Gr]

    decay = [jnp.exp(jnp.where(incl, Gc[h] - Gr[h], NEG_BIG)) for h in HEADS]
    KQ = [jnp.dot(jnp.concatenate([k_ref[:, hs[h]], q_ref[:, hs[h]]], axis=0), kT_ref[hs[h], :],
                  preferred_element_type=F32) for h in HEADS]
    A = [jnp.where(strict, beta[h] * KQ[h][:CH] * decay[h], 0.0) for h in HEADS]
    qkm = [(KQ[h][CH:] * decay[h]).astype(BF16) for h in HEADS]
    Dg = [jnp.where(m16, A[h], 0.0) for h in HEADS]
    D2 = [_dot(Dg[h], Dg[h]) for h in HEADS]
    X = [eye - Dg[h] for h in HEADS]
    Y = [_dot(_stack(D2[h], X[h]), D2[h]) for h in HEADS]
    D4 = [Y[h][:CH] for h in HEADS]
    X = [X[h] + Y[h][CH:] for h in HEADS]
    Y = [_dot(_stack(D4[h], X[h]), D4[h]) for h in HEADS]
    X = [X[h] + Y[h][CH:] for h in HEADS]
    X = [X[h] + _dot(X[h], Y[h][:CH]) for h in HEADS]
    for off in levels:
        Z = [_dot(X[h], jnp.where(off, A[h], 0.0)) for h in HEADS]
        X = [X[h] - _dot(Z[h], X[h]) for h in HEADS]

    eG = [jnp.exp(Gc[h]) for h in HEADS]
    rhs = [jnp.concatenate([(k_ref[:, hs[h]].astype(F32) * (beta[h] * eG[h])).astype(BF16),
                            (v_ref[:, hs[h]].astype(F32) * beta[h]).astype(BF16)], axis=-1)
           for h in HEADS]
    sol = [_dot(X[h], rhs[h]) for h in HEADS]
    R = [_dot(_stack(sol[h][:, :HEAD_DIM], q_ref[:, hs[h]].astype(F32) * eG[h]), S_ref[h])
         for h in HEADS]
    v_new = [sol[h][:, HEAD_DIM:] - R[h][:CH] for h in HEADS]
    keT = [(kT_ref[hs[h], :].astype(F32) * jnp.exp(G_end[h] - Gr[h])).astype(BF16) for h in HEADS]
    Y = [_dot(jnp.concatenate([qkm[h], keT[h]], axis=0), v_new[h]) for h in HEADS]
    for h in HEADS:
        o_ref[:, hs[h]] = (R[h][CH:] + Y[h][:CH]).astype(o_ref.dtype)
        S_ref[h] = S_ref[h] * jnp.exp(G_end[h]) + Y[h][CH:]


def _dn_scan(qn, kn, vc, knT, gcol, gT, seq_len, direction):
    T = qn.shape[0]
    CH = SCAN_CHUNK
    nc = seq_len // CH
    nb = T // seq_len
    if direction == 0:
        blk = lambda b, c: b * nc + c
    else:
        blk = lambda b, c: b * nc + (nc - 1 - c)
    tok_spec = pl.BlockSpec((CH, D_MODEL), lambda b, c: (blk(b, c), 0))
    return pl.pallas_call(
        functools.partial(_dn_scan_kernel, direction),
        grid=(nb, nc),
        in_specs=[
            tok_spec, tok_spec, tok_spec,
            pl.BlockSpec((D_MODEL, CH), lambda b, c: (0, blk(b, c))),
            pl.BlockSpec((CH, 128), lambda b, c: (blk(b, c), 0)),
            pl.BlockSpec((N_DIR * N_HEADS, CH), lambda b, c: (0, blk(b, c))),
        ],
        out_specs=tok_spec,
        out_shape=jax.ShapeDtypeStruct((T, D_MODEL), BF16),
        scratch_shapes=[pltpu.VMEM((N_HEADS, HEAD_DIM, HEAD_DIM), F32)],
        compiler_params=_params(("parallel", "arbitrary"), 32),
        name=f"dn_scan_{direction}",
    )(qn, kn, vc, knT, gcol, gT)


def _mix_kernel(u_ref, vs_ref, z_ref, ga_ref, gb_ref, of_ref, ob_ref, x_ref,
                lng_ref, lnb_ref, ws_ref, bs_ref, on_ref, wo_ref, out_ref, mix_ref):
    tm = x_ref.shape[0]
    v = _gelu(vs_ref[...].astype(F32))
    vcen = v - jnp.mean(v, axis=-1, keepdims=True)
    vln = vcen * lax.rsqrt(jnp.mean(vcen * vcen, axis=-1, keepdims=True) + EPS)
    vb = (vln * lng_ref[...] + lnb_ref[...]).astype(BF16)
    for g in range(SGU_GROUPS):
        gs = slice(g * HEAD_DIM, (g + 1) * HEAD_DIM)
        u = _gelu(u_ref[:, gs].astype(F32))
        gate_a = jax.nn.sigmoid(ga_ref[:, gs].astype(F32))
        o = of_ref[:, gs].astype(F32) + ob_ref[:, gs].astype(F32)
        o = o * lax.rsqrt(jnp.mean(o * o, axis=-1, keepdims=True) + EPS) * on_ref[...]
        o = o * _silu(z_ref[:, gs].astype(F32))
        mixed = jax.nn.sigmoid(gb_ref[:, gs].astype(F32)) * o
        w_sp = ws_ref[g]
        bias = bs_ref[:, g:g + 1]
        for c in range(tm // SGU_CHUNK):
            rs = slice(c * SGU_CHUNK, (c + 1) * SGU_CHUNK)
            s = jnp.dot(w_sp, vb[rs, gs], preferred_element_type=F32) + bias
            mix_ref[rs, gs] = (mixed[rs] + gate_a[rs] * (u[rs] * s)).astype(BF16)
    out_ref[...] = x_ref[...] + jnp.dot(mix_ref[...], wo_ref[...], preferred_element_type=F32)


def _mix(P, o_f, o_b, x, ln_g, ln_b, w_sp, b_spT, o_norm, w_out):
    T = x.shape[0]
    tm = min(256, T)
    col = lambda j: pl.BlockSpec((tm, D_MODEL), lambda i: (i, j))
    tok = pl.BlockSpec((tm, D_MODEL), lambda i: (i, 0))
    vec = pl.BlockSpec((1, D_MODEL), lambda i: (0, 0))
    return pl.pallas_call(
        _mix_kernel,
        grid=(T // tm,),
        in_specs=[
            col(3), col(4), col(5), col(6), col(7), tok, tok, tok, vec, vec,
            pl.BlockSpec((SGU_GROUPS, SGU_CHUNK, SGU_CHUNK), lambda i: (0, 0, 0)),
            pl.BlockSpec((SGU_CHUNK, SGU_GROUPS), lambda i: (0, 0)),
            pl.BlockSpec((1, HEAD_DIM), lambda i: (0, 0)),
            pl.BlockSpec((D_MODEL, D_MODEL), lambda i: (0, 0)),
        ],
        out_specs=tok,
        out_shape=jax.ShapeDtypeStruct((T, D_MODEL), F32),
        scratch_shapes=[pltpu.VMEM((tm, D_MODEL), BF16)],
        compiler_params=_params(("parallel",), 48),
        name="mix",
    )(P, P, P, P, P, o_f, o_b, x, ln_g, ln_b, w_sp, b_spT, o_norm, w_out)


def _ffn_kernel(x_ref, nw_ref, wg_ref, wu_ref, wd_ref, o_ref, h_ref):
    @pl.when(pl.program_id(1) == 0)
    def _():
        xf = x_ref[...]
        h_ref[...] = _rms(xf, nw_ref[...]).astype(BF16)
        o_ref[...] = xf

    h = h_ref[...]
    g = jnp.dot(h, wg_ref[...], preferred_element_type=F32)
    u = jnp.dot(h, wu_ref[...], preferred_element_type=F32)
    o_ref[...] += jnp.dot((_silu(g) * u).astype(BF16), wd_ref[...], preferred_element_type=F32)


def _ffn(x, nw, wg, wu, wd):
    T = x.shape[0]
    F = wg.shape[1]
    tm = min(1024, T)
    tf = 256
    tok = pl.BlockSpec((tm, D_MODEL), lambda i, f: (i, 0))
    return pl.pallas_call(
        _ffn_kernel,
        grid=(T // tm, F // tf),
        in_specs=[
            tok,
            pl.BlockSpec((1, D_MODEL), lambda i, f: (0, 0)),
            pl.BlockSpec((D_MODEL, tf), lambda i, f: (0, f)),
            pl.BlockSpec((D_MODEL, tf), lambda i, f: (0, f)),
            pl.BlockSpec((tf, D_MODEL), lambda i, f: (f, 0)),
        ],
        out_specs=tok,
        out_shape=jax.ShapeDtypeStruct((T, D_MODEL), F32),
        scratch_shapes=[pltpu.VMEM((tm, D_MODEL), BF16)],
        compiler_params=_params(("parallel", "arbitrary"), 48),
        name="ffn",
    )(x, nw, wg, wu, wd)


def _moe_kernel(nf, x_ref, nw_ref, wrh_ref, wrl_ref, wg_ref, wu_ref, wd_ref, nfin_ref, o_ref,
                h_ref, rank_ref, gate_ref, cnt_ref, xe_ref, ye_ref):
    e = pl.program_id(1)
    f = pl.program_id(2)
    Tm = x_ref.shape[0]
    RB = min(MOE_ROW_BLOCK, Tm)
    nt = (((1,), (1,)), ((), ()))
    tn = (((0,), (0,)), ((), ()))

    @pl.when((e == 0) & (f == 0))
    def _route():
        xf = x_ref[...]
        h = _rms(xf, nw_ref[...])
        hb = h.astype(BF16)
        h_ref[...] = hb
        hl = (h - hb.astype(F32)).astype(BF16)
        wrh = wrh_ref[...]
        logits = (lax.dot_general(wrh, hb, nt, preferred_element_type=F32)
                  + lax.dot_general(wrh, hl, nt, preferred_element_type=F32)
                  + lax.dot_general(wrl_ref[...], hb, nt, preferred_element_type=F32))
        ei = lax.broadcasted_iota(jnp.int32, logits.shape, 0)
        m1 = jnp.max(logits, axis=0, keepdims=True)
        i1 = jnp.min(jnp.where(logits == m1, ei, N_EXPERTS), axis=0, keepdims=True)
        s1 = ei == i1
        rest = jnp.where(s1, -jnp.inf, logits)
        m2 = jnp.max(rest, axis=0, keepdims=True)
        i2 = jnp.min(jnp.where(rest == m2, ei, N_EXPERTS), axis=0, keepdims=True)
        s2 = ei == i2
        e2 = jnp.exp(m2 - m1)
        w1 = 1.0 / (1.0 + e2)
        gate_ref[...] = jnp.where(s1, w1, 0.0) + jnp.where(s2, e2 * w1, 0.0)
        sel = jnp.where(s1 | s2, 1.0, 0.0)
        ri = lax.broadcasted_iota(jnp.int32, (Tm, Tm), 0)
        ci = lax.broadcasted_iota(jnp.int32, (Tm, Tm), 1)
        before = jnp.where(ri < ci, 1.0, 0.0).astype(BF16)
        rank = jnp.dot(sel.astype(BF16), before, preferred_element_type=F32)
        rank_ref[...] = jnp.where(sel > 0.0, rank, -1.0)
        cnt = jnp.sum(sel, axis=1, keepdims=True)
        for ee in range(N_EXPERTS):
            cnt_ref[ee] = cnt[ee, 0].astype(jnp.int32)
        o_ref[...] = xf

    n_blocks = (cnt_ref[e] + RB - 1) // RB
    rk = rank_ref[pl.ds(e, 1), :]

    def one_hot(rb, value):
        rows = (lax.broadcasted_iota(jnp.int32, (RB, Tm), 0) + rb * RB).astype(F32)
        return jnp.where(rk == rows, value, 0.0).astype(BF16)

    @pl.when(f == 0)
    def _gather():
        def body(rb, carry):
            rows = pl.ds(pl.multiple_of(rb * RB, RB), RB)
            xe_ref[rows, :] = jnp.dot(one_hot(rb, 1.0), h_ref[...],
                                      preferred_element_type=F32).astype(BF16)
            ye_ref[rows, :] = jnp.zeros((RB, D_MODEL), F32)
            return carry
        lax.fori_loop(0, n_blocks, body, 0)

    def expert_body(rb, carry):
        rows = pl.ds(pl.multiple_of(rb * RB, RB), RB)
        xb = xe_ref[rows, :]
        g = jnp.dot(xb, wg_ref[0], preferred_element_type=F32)
        u = jnp.dot(xb, wu_ref[0], preferred_element_type=F32)
        ye_ref[rows, :] += jnp.dot((_silu(g) * u).astype(BF16), wd_ref[0],
                                   preferred_element_type=F32)
        return carry
    lax.fori_loop(0, n_blocks, expert_body, 0)

    @pl.when(f == nf - 1)
    def _scatter():
        gt = gate_ref[pl.ds(e, 1), :]

        def body(rb, carry):
            rows = pl.ds(pl.multiple_of(rb * RB, RB), RB)
            o_ref[...] += lax.dot_general(one_hot(rb, gt), ye_ref[rows, :].astype(BF16), tn,
                                          preferred_element_type=F32)
            return carry
        lax.fori_loop(0, n_blocks, body, 0)

    @pl.when((e == N_EXPERTS - 1) & (f == nf - 1))
    def _final():
        o_ref[...] = _rms(o_ref[...], nfin_ref[...])


def _moe(x, nw, wr_hi, wr_lo, wg, wu, wd, n_final):
    T = x.shape[0]
    F = wg.shape[2]
    tm = min(1024, T)
    tf = 896
    nf = F // tf
    rb = min(MOE_ROW_BLOCK, tm)
    cap = -(-tm // rb) * rb
    tok = pl.BlockSpec((tm, D_MODEL), lambda i, e, f: (i, 0))
    vec = pl.BlockSpec((1, D_MODEL), lambda i, e, f: (0, 0))
    wr = pl.BlockSpec((N_EXPERTS, D_MODEL), lambda i, e, f: (0, 0))
    return pl.pallas_call(
        functools.partial(_moe_kernel, nf),
        grid=(T // tm, N_EXPERTS, nf),
        in_specs=[
            tok, vec, wr, wr,
            pl.BlockSpec((1, D_MODEL, tf), lambda i, e, f: (e, 0, f)),
            pl.BlockSpec((1, D_MODEL, tf), lambda i, e, f: (e, 0, f)),
            pl.BlockSpec((1, tf, D_MODEL), lambda i, e, f: (e, f, 0)),
            vec,
        ],
        out_specs=tok,
        out_shape=jax.ShapeDtypeStruct((T, D_MODEL), F32),
        scratch_shapes=[
            pltpu.VMEM((tm, D_MODEL), BF16),
            pltpu.VMEM((N_EXPERTS, tm), F32),
            pltpu.VMEM((N_EXPERTS, tm), F32),
            pltpu.SMEM((N_EXPERTS,), jnp.int32),
            pltpu.VMEM((cap, D_MODEL), BF16),
            pltpu.VMEM((cap, D_MODEL), F32),
        ],
        compiler_params=_params(("parallel", "arbitrary", "arbitrary"), 56),
        name="moe",
    )(x, nw, wr_hi, wr_lo, wg, wu, wd, n_final)


def _pad_lanes(v, width=128):
    v = v.reshape(1, -1).astype(F32)
    return jnp.pad(v, ((0, 0), (0, width - v.shape[1])))


def _prepare(norm_mix, w_in, sgu_ln_gain, sgu_ln_bias, sgu_w_spatial, sgu_b_spatial,
             dn_conv_w, dn_a_log, dn_dt_bias, dn_out_norm, w_out, norm_ffn,
             ffn_w_gate, ffn_w_up, ffn_w_down, moe_w_router, moe_w_gate, moe_w_up,
             moe_w_down, norm_final):
    depth = w_in.shape[0]
    W = D_MODEL
    layers = []
    for i in range(depth):
        wi = w_in[i]
        seg = lambda a, b: wi[:, a:b]
        ab0 = 6 * W
        ab1 = ab0 + 2 * N_DIR * N_HEADS
        w_main = jnp.concatenate(
            [seg(2 * W, 5 * W), seg(0, 2 * W), seg(5 * W, 6 * W), seg(ab1, ab1 + 2 * W)],
            axis=1).astype(BF16)
        w_ab = jnp.pad(seg(ab0, ab1), ((0, 0), (0, 128 - (ab1 - ab0)))).astype(BF16)
        lp = dict(
            norm_mix=norm_mix[i].reshape(1, W), w_main=w_main, w_ab=w_ab,
            ln_g=sgu_ln_gain[i].reshape(1, W), ln_b=sgu_ln_bias[i].reshape(1, W),
            w_sp=sgu_w_spatial[i].astype(BF16), b_spT=jnp.transpose(sgu_b_spatial[i]),
            conv_w=dn_conv_w[i],
            neg_exp_alog=_pad_lanes(-jnp.exp(dn_a_log[i].astype(F32))),
            dt_bias=_pad_lanes(dn_dt_bias[i]),
            o_norm=dn_out_norm[i].reshape(1, HEAD_DIM), w_out=w_out[i].astype(BF16),
            norm_ffn=norm_ffn[i].reshape(1, W),
        )
        j = i // 2
        if i % 2 == 0:
            lp.update(wg=ffn_w_gate[j].astype(BF16), wu=ffn_w_up[j].astype(BF16),
                      wd=ffn_w_down[j].astype(BF16))
        else:
            wrT = jnp.transpose(moe_w_router[j]).astype(F32)
            wr_hi = wrT.astype(BF16)
            lp.update(wr_hi=wr_hi, wr_lo=(wrT - wr_hi.astype(F32)).astype(BF16),
                      wg=moe_w_gate[j].astype(BF16), wu=moe_w_up[j].astype(BF16),
                      wd=moe_w_down[j].astype(BF16))
        layers.append(lp)
    return layers, norm_final.reshape(1, W)


def _trunk(x3, layers, n_final):
    B, L, W = x3.shape
    x = x3.reshape(B * L, W)
    depth = len(layers)
    assert depth % 2 == 0, "the final RMSNorm is fused into the last (expert) layer"
    for i, lp in enumerate(layers):
        P, ab = _in_proj(x, lp["norm_mix"], lp["w_main"], lp["w_ab"])
        qn, kn, vc, knT, gcol, gT = _dn_prep(P, ab, lp["conv_w"], lp["neg_exp_alog"],
                                             lp["dt_bias"], L)
        o_f = _dn_scan(qn, kn, vc, knT, gcol, gT, L, 0)
        o_b = _dn_scan(qn, kn, vc, knT, gcol, gT, L, 1)
        x = _mix(P, o_f, o_b, x, lp["ln_g"], lp["ln_b"], lp["w_sp"], lp["b_spT"],
                 lp["o_norm"], lp["w_out"])
        if i % 2 == 0:
            x = _ffn(x, lp["norm_ffn"], lp["wg"], lp["wu"], lp["wd"])
        else:
            assert i == depth - 1
            x = _moe(x, lp["norm_ffn"], lp["wr_hi"], lp["wr_lo"], lp["wg"], lp["wu"], lp["wd"],
                     n_final)
    return x.reshape(B, L, W)


def kernel(x_prompt, x_sample, norm_mix, w_in, sgu_ln_gain, sgu_ln_bias, sgu_w_spatial, sgu_b_spatial, dn_conv_w, dn_a_log, dn_dt_bias, dn_out_norm, w_out, norm_ffn, ffn_w_gate, ffn_w_up, ffn_w_down, moe_w_router, moe_w_gate, moe_w_up, moe_w_down, norm_final):
    layers, n_final = _prepare(norm_mix, w_in, sgu_ln_gain, sgu_ln_bias, sgu_w_spatial,
                               sgu_b_spatial, dn_conv_w, dn_a_log, dn_dt_bias, dn_out_norm,
                               w_out, norm_ffn, ffn_w_gate, ffn_w_up, ffn_w_down, moe_w_router,
                               moe_w_gate, moe_w_up, moe_w_down, norm_final)
    return (_trunk(x_prompt, layers, n_final), _trunk(x_sample, layers, n_final))
```

```python
import functools

import jax
import jax.numpy as jnp
from jax import lax
from jax.experimental import pallas as pl
from jax.experimental.pallas import tpu as pltpu

F32 = jnp.float32
BF16 = jnp.bfloat16
EPS = 1e-6

D_MODEL = 1024
N_HEADS = 8
HEAD_DIM = 128
N_DIR = 2
CONV_K = 5
CONV_HALO = 8
HALO_BLOCK = 16
SGU_CHUNK = 128
SGU_GROUPS = 8
N_EXPERTS = 8
SCAN_CHUNK = 128
MOE_ROW_BLOCK = 288
NEG_BIG = -1e30


def _params(sem, vmem_mb):
    return pltpu.CompilerParams(dimension_semantics=sem, vmem_limit_bytes=vmem_mb * 2**20)


def _dot(a, b):
    return jnp.dot(a.astype(BF16), b.astype(BF16), preferred_element_type=F32)


def _rms(xf, w):
    return xf * lax.rsqrt(jnp.mean(xf * xf, axis=-1, keepdims=True) + EPS) * w


def _silu(x):
    return x * jax.nn.sigmoid(x)


def _gelu(x):
    return 0.5 * x * (1.0 + lax.erf(x * (2.0 ** -0.5)))


def _in_proj_kernel(x_ref, nw_ref, w_ref, wab_ref, p_ref, ab_ref, h_ref):
    @pl.when(pl.program_id(1) == 0)
    def _():
        hb = _rms(x_ref[...], nw_ref[...]).astype(BF16)
        h_ref[...] = hb
        ab_ref[...] = jnp.dot(hb, wab_ref[...], preferred_element_type=F32)

    p_ref[...] = jnp.dot(h_ref[...], w_ref[...], preferred_element_type=F32).astype(BF16)


def _in_proj(x, nw, w_main, w_ab):
    T = x.shape[0]
    tm = min(1024, T)
    tn = 1024
    n_cols = w_main.shape[1]
    return pl.pallas_call(
        _in_proj_kernel,
        grid=(T // tm, n_cols // tn),
        in_specs=[
            pl.BlockSpec((tm, D_MODEL), lambda i, j: (i, 0)),
            pl.BlockSpec((1, D_MODEL), lambda i, j: (0, 0)),
            pl.BlockSpec((D_MODEL, tn), lambda i, j: (0, j)),
            pl.BlockSpec((D_MODEL, 128), lambda i, j: (0, 0)),
        ],
        out_specs=[
            pl.BlockSpec((tm, tn), lambda i, j: (i, j)),
            pl.BlockSpec((tm, 128), lambda i, j: (i, 0)),
        ],
        out_shape=[
            jax.ShapeDtypeStruct((T, n_cols), BF16),
            jax.ShapeDtypeStruct((T, 128), F32),
        ],
        scratch_shapes=[pltpu.VMEM((tm, D_MODEL), BF16)],
        compiler_params=_params(("parallel", "arbitrary"), 48),
        name="in_proj",
    )(x, nw, w_main, w_ab)


def _dn_prep_kernel(tiles_per_seq, pm_ref, pp_ref, pn_ref, ab_ref, cw_ref, nea_ref, dtb_ref,
                    qn_ref, kn_ref, vc_ref, knT_ref, gcol_ref, gT_ref, xp_ref):
    tl = pm_ref.shape[0]
    pos = pl.program_id(0) % tiles_per_seq
    keep_prev = jnp.where(pos == 0, 0.0, 1.0)
    keep_next = jnp.where(pos == tiles_per_seq - 1, 0.0, 1.0)
    q_scale = HEAD_DIM ** -0.5

    for s in range(3):
        cs = slice(s * D_MODEL, (s + 1) * D_MODEL)
        prev = pp_ref[:, cs].astype(F32)[HALO_BLOCK - CONV_HALO:, :] * keep_prev
        nxt = pn_ref[:, cs].astype(F32)[:CONV_HALO, :] * keep_next
        xp_ref[0:CONV_HALO, :] = prev
        xp_ref[CONV_HALO:CONV_HALO + tl, :] = pm_ref[:, cs].astype(F32)
        xp_ref[CONV_HALO + tl:2 * CONV_HALO + tl, :] = nxt
        acc = None
        for k in range(CONV_K):
            start = CONV_HALO - CONV_K // 2 + k
            term = xp_ref[start:start + tl, :] * cw_ref[k:k + 1, cs]
            acc = term if acc is None else acc + term
        y = _silu(acc)
        if s == 2:
            vc_ref[...] = y.astype(BF16)
            continue
        for h in range(N_HEADS):
            hs = slice(h * HEAD_DIM, (h + 1) * HEAD_DIM)
            slab = y[:, hs]
            slab = slab * lax.rsqrt(jnp.sum(slab * slab, axis=-1, keepdims=True) + EPS)
            if s == 0:
                qn_ref[:, hs] = (slab * q_scale).astype(BF16)
            else:
                kn_ref[:, hs] = slab.astype(BF16)
                knT_ref[hs, :] = jnp.transpose(slab).astype(BF16)

    ab = ab_ref[...]
    z = ab + dtb_ref[...]
    softplus = jnp.maximum(z, 0.0) + jnp.log(1.0 + jnp.exp(-jnp.abs(z)))
    g = nea_ref[...] * softplus
    beta = jax.nn.sigmoid(ab)
    ri = lax.broadcasted_iota(jnp.int32, (tl, tl), 0)
    ci = lax.broadcasted_iota(jnp.int32, (tl, tl), 1)
    same = jnp.right_shift(ri, 7) == jnp.right_shift(ci, 7)
    lower = jnp.where(same & (ci <= ri), 1.0, 0.0).astype(BF16)
    upper = jnp.where(same & (ci >= ri), 1.0, 0.0).astype(BF16)
    g_hi = g.astype(BF16)
    g_lo = (g - g_hi.astype(F32)).astype(BF16)
    g_fwd = (jnp.dot(lower, g_hi, preferred_element_type=F32)
             + jnp.dot(lower, g_lo, preferred_element_type=F32))
    g_bwd = (jnp.dot(upper, g_hi, preferred_element_type=F32)
             + jnp.dot(upper, g_lo, preferred_element_type=F32))
    lane = lax.broadcasted_iota(jnp.int32, g.shape, 1)
    gcol = jnp.where(lane < N_HEADS, g_fwd, jnp.where(lane < N_DIR * N_HEADS, g_bwd, beta))
    gcol_ref[...] = gcol
    gT_ref[...] = jnp.transpose(gcol)[0:N_DIR * N_HEADS, :]


def _dn_prep(P, ab, conv_w, neg_exp_alog, dt_bias, seq_len):
    T = P.shape[0]
    tl = min(256, seq_len)
    tiles_per_seq = seq_len // tl
    hb = tl // HALO_BLOCK
    n_hb = T // HALO_BLOCK
    qkv_w = 3 * D_MODEL
    tok = jax.ShapeDtypeStruct((T, D_MODEL), BF16)
    return pl.pallas_call(
        functools.partial(_dn_prep_kernel, tiles_per_seq),
        grid=(T // tl,),
        in_specs=[
            pl.BlockSpec((tl, qkv_w), lambda i: (i, 0)),
            pl.BlockSpec((HALO_BLOCK, qkv_w), lambda i: (jnp.maximum(i * hb - 1, 0), 0)),
            pl.BlockSpec((HALO_BLOCK, qkv_w), lambda i: (jnp.minimum((i + 1) * hb, n_hb - 1), 0)),
            pl.BlockSpec((tl, 128), lambda i: (i, 0)),
            pl.BlockSpec((CONV_K, qkv_w), lambda i: (0, 0)),
            pl.BlockSpec((1, 128), lambda i: (0, 0)),
            pl.BlockSpec((1, 128), lambda i: (0, 0)),
        ],
        out_specs=[
            pl.BlockSpec((tl, D_MODEL), lambda i: (i, 0)),
            pl.BlockSpec((tl, D_MODEL), lambda i: (i, 0)),
            pl.BlockSpec((tl, D_MODEL), lambda i: (i, 0)),
            pl.BlockSpec((D_MODEL, tl), lambda i: (0, i)),
            pl.BlockSpec((tl, 128), lambda i: (i, 0)),
            pl.BlockSpec((N_DIR * N_HEADS, tl), lambda i: (0, i)),
        ],
        out_shape=[tok, tok, tok,
                   jax.ShapeDtypeStruct((D_MODEL, T), BF16),
                   jax.ShapeDtypeStruct((T, 128), F32),
                   jax.ShapeDtypeStruct((N_DIR * N_HEADS, T), F32)],
        scratch_shapes=[pltpu.VMEM((tl + 2 * CONV_HALO, D_MODEL), F32)],
        compiler_params=_params(("parallel",), 48),
        name="dn_prep",
    )(P, P, P, ab, conv_w, neg_exp_alog, dt_bias)


def _stack(a, b):
    return jnp.concatenate([a.astype(BF16), b.astype(BF16)], axis=0)


def _merge_rows(direction, blk):
    first = blk if direction == 0 else 0
    return [slice(s, s + blk) for s in range(first, SCAN_CHUNK, 2 * blk)]


def _dn_scan_kernel(qf_ref, kf_ref, vf_ref, kTf_ref, gcf_ref, gTf_ref,
                    qb_ref, kb_ref, vb_ref, kTb_ref, gcb_ref, gTb_ref,
                    of_ref, ob_ref, S_ref):
    CH = SCAN_CHUNK
    refs = ((qf_ref, kf_ref, vf_ref, kTf_ref, gcf_ref, gTf_ref, of_ref),
            (qb_ref, kb_ref, vb_ref, kTb_ref, gcb_ref, gTb_ref, ob_ref))
    chains = [(d, h) for d in range(N_DIR) for h in range(N_HEADS)]
    C = range(len(chains))

    @pl.when(pl.program_id(1) == 0)
    def _():
        S_ref[...] = jnp.zeros_like(S_ref)

    ri = lax.broadcasted_iota(jnp.int32, (CH, CH), 0)
    ci = lax.broadcasted_iota(jnp.int32, (CH, CH), 1)
    incl = (ri >= ci, ri <= ci)
    strict = (ri > ci, ri < ci)
    end = (CH - 1, 0)
    eye = jnp.where(ri == ci, 1.0, 0.0)
    same = lambda sh: jnp.right_shift(ri, sh) == jnp.right_shift(ci, sh)
    m16, m32, m64 = same(4), same(5), same(6)
    levels = ((16, m32 & ~m16), (32, m64 & ~m32), (64, ~m64))
    hs = lambda h: slice(h * HEAD_DIM, (h + 1) * HEAD_DIM)
    gc = [r[4][...] for r in refs]
    gT = [r[5][...] for r in refs]

    def col(d, h, base):
        l = base + d * N_HEADS + h
        return gc[d][:, l:l + 1]

    Gc = [col(d, h, 0) for d, h in chains]
    beta = [col(d, h, N_DIR * N_HEADS) for d, h in chains]
    Gr = [gT[d][d * N_HEADS + h:d * N_HEADS + h + 1, :] for d, h in chains]
    G_end = [Gr[c][:, end[chains[c][0]]:end[chains[c][0]] + 1] for c in C]
    q = [refs[d][0][:, hs(h)] for d, h in chains]
    k = [refs[d][1][:, hs(h)] for d, h in chains]
    kT = [refs[d][3][hs(h), :] for d, h in chains]

    decay = [jnp.exp(jnp.where(incl[chains[c][0]], Gc[c] - Gr[c], NEG_BIG)) for c in C]
    KQ = [jnp.dot(jnp.concatenate([k[c], q[c]], axis=0), kT[c], preferred_element_type=F32) for c in C]
    A = [jnp.where(strict[chains[c][0]], beta[c] * KQ[c][:CH] * decay[c], 0.0) for c in C]
    qkm = [(KQ[c][CH:] * decay[c]).astype(BF16) for c in C]
    Dg = [jnp.where(m16, A[c], 0.0) for c in C]
    D2 = [_dot(Dg[c], Dg[c]) for c in C]
    X = [eye - Dg[c] for c in C]
    Y = [_dot(_stack(D2[c], X[c]), D2[c]) for c in C]
    D4 = [Y[c][:CH] for c in C]
    X = [X[c] + Y[c][CH:] for c in C]
    Y = [_dot(_stack(D4[c], X[c]), D4[c]) for c in C]
    X = [X[c] + Y[c][CH:] for c in C]
    X = [X[c] + _dot(X[c], Y[c][:CH]) for c in C]
    for blk, off in levels:
        rows = [_merge_rows(chains[c][0], blk) for c in C]
        Xr = [jnp.concatenate([X[c][r] for r in rows[c]], axis=0) for c in C]
        Z = [_dot(Xr[c], jnp.where(off, A[c], 0.0)) for c in C]
        Xr = [Xr[c] - _dot(Z[c], X[c]) for c in C]
        Xn = []
        for c in C:
            parts, pos = [], 0
            for j, r in enumerate(rows[c]):
                if r.start > pos:
                    parts.append(X[c][pos:r.start])
                parts.append(Xr[c][j * blk:(j + 1) * blk])
                pos = r.stop
            if pos < CH:
                parts.append(X[c][pos:CH])
            Xn.append(jnp.concatenate(parts, axis=0))
        X = Xn

    eG = [jnp.exp(Gc[c]) for c in C]
    S = [S_ref[c] for c in C]
    Sb = [S[c].astype(BF16) for c in C]
    kS = [jnp.dot((k[c].astype(F32) * (beta[c] * eG[c])).astype(BF16), Sb[c],
                  preferred_element_type=F32) for c in C]
    v_new = [_dot(X[c], refs[chains[c][0]][2][:, hs(chains[c][1])].astype(F32) * beta[c] - kS[c])
             for c in C]
    vnb = [v_new[c].astype(BF16) for c in C]
    o = [jnp.dot(jnp.concatenate([(q[c].astype(F32) * eG[c]).astype(BF16), qkm[c]], axis=1),
                 jnp.concatenate([Sb[c], vnb[c]], axis=0), preferred_element_type=F32) for c in C]
    dS = [jnp.dot((kT[c].astype(F32) * jnp.exp(G_end[c] - Gr[c])).astype(BF16), vnb[c],
                  preferred_element_type=F32) for c in C]
    for c in C:
        d, h = chains[c]
        refs[d][6][:, hs(h)] = o[c].astype(refs[d][6].dtype)
        S_ref[c] = S[c] * jnp.exp(G_end[c]) + dS[c]


def _dn_scan(qn, kn, vc, knT, gcol, gT, seq_len):
    T = qn.shape[0]
    CH = SCAN_CHUNK
    nc = seq_len // CH
    nb = T // seq_len
    fwd = lambda b, c: b * nc + c
    bwd = lambda b, c: b * nc + (nc - 1 - c)

    def specs(blk):
        tok = pl.BlockSpec((CH, D_MODEL), lambda b, c: (blk(b, c), 0))
        return [tok, tok, tok,
                pl.BlockSpec((D_MODEL, CH), lambda b, c: (0, blk(b, c))),
                pl.BlockSpec((CH, 128), lambda b, c: (blk(b, c), 0)),
                pl.BlockSpec((N_DIR * N_HEADS, CH), lambda b, c: (0, blk(b, c)))]

    out = jax.ShapeDtypeStruct((T, D_MODEL), BF16)
    args = (qn, kn, vc, knT, gcol, gT)
    return pl.pallas_call(
        _dn_scan_kernel,
        grid=(nb, nc),
        in_specs=specs(fwd) + specs(bwd),
        out_specs=[specs(fwd)[0], specs(bwd)[0]],
        out_shape=[out, out],
        scratch_shapes=[pltpu.VMEM((N_DIR * N_HEADS, HEAD_DIM, HEAD_DIM), F32)],
        compiler_params=_params(("parallel", "arbitrary"), 32),
        name="dn_scan",
    )(*args, *args)


def _mix_kernel(u_ref, vs_ref, z_ref, ga_ref, gb_ref, of_ref, ob_ref, x_ref,
                lng_ref, lnb_ref, ws_ref, bs_ref, on_ref, wo_ref, out_ref, mix_ref):
    tm = x_ref.shape[0]
    v = _gelu(vs_ref[...].astype(F32))
    vcen = v - jnp.mean(v, axis=-1, keepdims=True)
    vln = vcen * lax.rsqrt(jnp.mean(vcen * vcen, axis=-1, keepdims=True) + EPS)
    vb = (vln * lng_ref[...] + lnb_ref[...]).astype(BF16)
    for g in range(SGU_GROUPS):
        gs = slice(g * HEAD_DIM, (g + 1) * HEAD_DIM)
        u = _gelu(u_ref[:, gs].astype(F32))
        gate_a = jax.nn.sigmoid(ga_ref[:, gs].astype(F32))
        o = of_ref[:, gs].astype(F32) + ob_ref[:, gs].astype(F32)
        o = o * lax.rsqrt(jnp.mean(o * o, axis=-1, keepdims=True) + EPS) * on_ref[...]
        o = o * _silu(z_ref[:, gs].astype(F32))
        mixed = jax.nn.sigmoid(gb_ref[:, gs].astype(F32)) * o
        w_sp = ws_ref[g]
        bias = bs_ref[:, g:g + 1]
        for c in range(tm // SGU_CHUNK):
            rs = slice(c * SGU_CHUNK, (c + 1) * SGU_CHUNK)
            s = jnp.dot(w_sp, vb[rs, gs], preferred_element_type=F32) + bias
            mix_ref[rs, gs] = (mixed[rs] + gate_a[rs] * (u[rs] * s)).astype(BF16)
    out_ref[...] = x_ref[...] + jnp.dot(mix_ref[...], wo_ref[...], preferred_element_type=F32)


def _mix(P, o_f, o_b, x, ln_g, ln_b, w_sp, b_spT, o_norm, w_out):
    T = x.shape[0]
    tm = min(256, T)
    col = lambda j: pl.BlockSpec((tm, D_MODEL), lambda i: (i, j))
    tok = pl.BlockSpec((tm, D_MODEL), lambda i: (i, 0))
    vec = pl.BlockSpec((1, D_MODEL), lambda i: (0, 0))
    return pl.pallas_call(
        _mix_kernel,
        grid=(T // tm,),
        in_specs=[
            col(3), col(4), col(5), col(6), col(7), tok, tok, tok, vec, vec,
            pl.BlockSpec((SGU_GROUPS, SGU_CHUNK, SGU_CHUNK), lambda i: (0, 0, 0)),
            pl.BlockSpec((SGU_CHUNK, SGU_GROUPS), lambda i: (0, 0)),
            pl.BlockSpec((1, HEAD_DIM), lambda i: (0, 0)),
            pl.BlockSpec((D_MODEL, D_MODEL), lambda i: (0, 0)),
        ],
        out_specs=tok,
        out_shape=jax.ShapeDtypeStruct((T, D_MODEL), F32),
        scratch_shapes=[pltpu.VMEM((tm, D_MODEL), BF16)],
        compiler_params=_params(("parallel",), 48),
        name="mix",
    )(P, P, P, P, P, o_f, o_b, x, ln_g, ln_b, w_sp, b_spT, o_norm, w_out)


def _ffn_kernel(x_ref, nw_ref, wg_ref, wu_ref, wd_ref, o_ref, h_ref):
    @pl.when(pl.program_id(1) == 0)
    def _():
        xf = x_ref[...]
        h_ref[...] = _rms(xf, nw_ref[...]).astype(BF16)
        o_ref[...] = xf

    h = h_ref[...]
    g = jnp.dot(h, wg_ref[...], preferred_element_type=F32)
    u = jnp.dot(h, wu_ref[...], preferred_element_type=F32)
    o_ref[...] += jnp.dot((_silu(g) * u).astype(BF16), wd_ref[...], preferred_element_type=F32)


def _ffn(x, nw, wg, wu, wd):
    T = x.shape[0]
    F = wg.shape[1]
    tm = min(1024, T)
    tf = 256
    tok = pl.BlockSpec((tm, D_MODEL), lambda i, f: (i, 0))
    return pl.pallas_call(
        _ffn_kernel,
        grid=(T // tm, F // tf),
        in_specs=[
            tok,
            pl.BlockSpec((1, D_MODEL), lambda i, f: (0, 0)),
            pl.BlockSpec((D_MODEL, tf), lambda i, f: (0, f)),
            pl.BlockSpec((D_MODEL, tf), lambda i, f: (0, f)),
            pl.BlockSpec((tf, D_MODEL), lambda i, f: (f, 0)),
        ],
        out_specs=tok,
        out_shape=jax.ShapeDtypeStruct((T, D_MODEL), F32),
        scratch_shapes=[pltpu.VMEM((tm, D_MODEL), BF16)],
        compiler_params=_params(("parallel", "arbitrary"), 48),
        name="ffn",
    )(x, nw, wg, wu, wd)


def _moe_kernel(nf, ns, x_ref, nw_ref, wrh_ref, wrl_ref, wg_ref, wu_ref, wd_ref, nfin_ref, o_ref,
                h_ref, rank_ref, gate_ref, cnt_ref, xe_ref, ye_ref):
    e = pl.program_id(1)
    f = pl.program_id(2)
    Ts = x_ref.shape[0] // ns
    RB = min(MOE_ROW_BLOCK, Ts)
    nt = (((1,), (1,)), ((), ()))
    tn = (((0,), (0,)), ((), ()))

    @pl.when((e == 0) & (f == 0))
    def _route():
        ri = lax.broadcasted_iota(jnp.int32, (Ts, Ts), 0)
        ci = lax.broadcasted_iota(jnp.int32, (Ts, Ts), 1)
        before = jnp.where(ri < ci, 1.0, 0.0).astype(BF16)
        wrh = wrh_ref[...]
        for s in range(ns):
            tok = slice(s * Ts, (s + 1) * Ts)
            xf = x_ref[tok, :]
            h = _rms(xf, nw_ref[...])
            hb = h.astype(BF16)
            h_ref[tok, :] = hb
            hl = (h - hb.astype(F32)).astype(BF16)
            logits = (lax.dot_general(wrh, hb, nt, preferred_element_type=F32)
                      + lax.dot_general(wrh, hl, nt, preferred_element_type=F32)
                      + lax.dot_general(wrl_ref[...], hb, nt, preferred_element_type=F32))
            ei = lax.broadcasted_iota(jnp.int32, logits.shape, 0)
            m1 = jnp.max(logits, axis=0, keepdims=True)
            i1 = jnp.min(jnp.where(logits == m1, ei, N_EXPERTS), axis=0, keepdims=True)
            s1 = ei == i1
            rest = jnp.where(s1, -jnp.inf, logits)
            m2 = jnp.max(rest, axis=0, keepdims=True)
            i2 = jnp.min(jnp.where(rest == m2, ei, N_EXPERTS), axis=0, keepdims=True)
            s2 = ei == i2
            e2 = jnp.exp(m2 - m1)
            w1 = 1.0 / (1.0 + e2)
            gate_ref[s] = jnp.where(s1, w1, 0.0) + jnp.where(s2, e2 * w1, 0.0)
            sel = jnp.where(s1 | s2, 1.0, 0.0)
            rank = jnp.dot(sel.astype(BF16), before, preferred_element_type=F32)
            rank_ref[s] = jnp.where(sel > 0.0, rank, -1.0)
            cnt = jnp.sum(sel, axis=1, keepdims=True)
            for ee in range(N_EXPERTS):
                cnt_ref[s * N_EXPERTS + ee] = cnt[ee, 0].astype(jnp.int32)
            o_ref[tok, :] = xf

    for s in range(ns):
        tok = slice(s * Ts, (s + 1) * Ts)
        n_blocks = (cnt_ref[s * N_EXPERTS + e] + RB - 1) // RB
        rk = rank_ref[s, pl.ds(e, 1), :]

        def one_hot(rb, value, rk=rk):
            rows = (lax.broadcasted_iota(jnp.int32, (RB, Ts), 0) + rb * RB).astype(F32)
            return jnp.where(rk == rows, value, 0.0).astype(BF16)

        @pl.when(f == 0)
        def _gather(s=s, tok=tok, n_blocks=n_blocks, one_hot=one_hot):
            def body(rb, carry):
                rows = pl.ds(pl.multiple_of(rb * RB, RB), RB)
                xe_ref[s, rows, :] = jnp.dot(one_hot(rb, 1.0), h_ref[tok, :],
                                             preferred_element_type=F32).astype(BF16)
                ye_ref[s, rows, :] = jnp.zeros((RB, D_MODEL), F32)
                return carry
            lax.fori_loop(0, n_blocks, body, 0)

        def expert_body(rb, carry, s=s):
            rows = pl.ds(pl.multiple_of(rb * RB, RB), RB)
            xb = xe_ref[s, rows, :]
            g = jnp.dot(xb, wg_ref[0, 0], preferred_element_type=F32)
            u = jnp.dot(xb, wu_ref[0, 0], preferred_element_type=F32)
            ye_ref[s, rows, :] += jnp.dot((_silu(g) * u).astype(BF16), wd_ref[0],
                                          preferred_element_type=F32)
            return carry
        lax.fori_loop(0, n_blocks, expert_body, 0)

        @pl.when(f == nf - 1)
        def _scatter(s=s, tok=tok, n_blocks=n_blocks, one_hot=one_hot):
            gt = gate_ref[s, pl.ds(e, 1), :]

            def body(rb, carry):
                rows = pl.ds(pl.multiple_of(rb * RB, RB), RB)
                o_ref[tok, :] += lax.dot_general(one_hot(rb, gt), ye_ref[s, rows, :].astype(BF16),
                                                 tn, preferred_element_type=F32)
                return carry
            lax.fori_loop(0, n_blocks, body, 0)

    @pl.when((e == N_EXPERTS - 1) & (f == nf - 1))
    def _final():
        o_ref[...] = _rms(o_ref[...], nfin_ref[...])


MOE_SUB_TILE = 1024
MOE_SUB_TILES = 2
MOE_FFN_BLOCK = 896


def _moe(x, nw, wr_hi, wr_lo, wg, wu, wd, n_final):
    T = x.shape[0]
    tf = MOE_FFN_BLOCK
    nf = wg.shape[1]
    ts = min(MOE_SUB_TILE, T)
    ns = min(MOE_SUB_TILES, T // ts)
    tm = ns * ts
    rb = min(MOE_ROW_BLOCK, ts)
    cap = -(-ts // rb) * rb
    tok = pl.BlockSpec((tm, D_MODEL), lambda i, e, f: (i, 0), pipeline_mode=pl.Buffered(1))
    vec = pl.BlockSpec((1, D_MODEL), lambda i, e, f: (0, 0))
    wr = pl.BlockSpec((N_EXPERTS, D_MODEL), lambda i, e, f: (0, 0))
    w_in = pl.BlockSpec((1, 1, D_MODEL, tf), lambda i, e, f: (e, f, 0, 0))
    return pl.pallas_call(
        functools.partial(_moe_kernel, nf, ns),
        grid=(T // tm, N_EXPERTS, nf),
        in_specs=[
            tok, vec, wr, wr, w_in, w_in,
            pl.BlockSpec((1, tf, D_MODEL), lambda i, e, f: (e, f, 0)),
            vec,
        ],
        out_specs=tok,
        out_shape=jax.ShapeDtypeStruct((T, D_MODEL), F32),
        scratch_shapes=[
            pltpu.VMEM((tm, D_MODEL), BF16),
            pltpu.VMEM((ns, N_EXPERTS, ts), F32),
            pltpu.VMEM((ns, N_EXPERTS, ts), F32),
            pltpu.SMEM((ns * N_EXPERTS,), jnp.int32),
            pltpu.VMEM((ns, cap, D_MODEL), BF16),
            pltpu.VMEM((ns, cap, D_MODEL), F32),
        ],
        compiler_params=_params(("parallel", "arbitrary", "arbitrary"), 58),
        name="moe",
    )(x, nw, wr_hi, wr_lo, wg, wu, wd, n_final)


def _pad_lanes(v, width=128):
    v = v.reshape(1, -1).astype(F32)
    return jnp.pad(v, ((0, 0), (0, width - v.shape[1])))


def _prepare(norm_mix, w_in, sgu_ln_gain, sgu_ln_bias, sgu_w_spatial, sgu_b_spatial,
             dn_conv_w, dn_a_log, dn_dt_bias, dn_out_norm, w_out, norm_ffn,
             ffn_w_gate, ffn_w_up, ffn_w_down, moe_w_router, moe_w_gate, moe_w_up,
             moe_w_down, norm_final):
    depth = w_in.shape[0]
    W = D_MODEL
    layers = []
    for i in range(depth):
        wi = w_in[i]
        seg = lambda a, b: wi[:, a:b]
        ab0 = 6 * W
        ab1 = ab0 + 2 * N_DIR * N_HEADS
        w_main = jnp.concatenate(
            [seg(2 * W, 5 * W), seg(0, 2 * W), seg(5 * W, 6 * W), seg(ab1, ab1 + 2 * W)],
            axis=1).astype(BF16)
        w_ab = jnp.pad(seg(ab0, ab1), ((0, 0), (0, 128 - (ab1 - ab0)))).astype(BF16)
        lp = dict(
            norm_mix=norm_mix[i].reshape(1, W), w_main=w_main, w_ab=w_ab,
            ln_g=sgu_ln_gain[i].reshape(1, W), ln_b=sgu_ln_bias[i].reshape(1, W),
            w_sp=sgu_w_spatial[i].astype(BF16), b_spT=jnp.transpose(sgu_b_spatial[i]),
            conv_w=dn_conv_w[i],
            neg_exp_alog=_pad_lanes(-jnp.exp(dn_a_log[i].astype(F32))),
            dt_bias=_pad_lanes(dn_dt_bias[i]),
            o_norm=dn_out_norm[i].reshape(1, HEAD_DIM), w_out=w_out[i].astype(BF16),
            norm_ffn=norm_ffn[i].reshape(1, W),
        )
        j = i // 2
        if i % 2 == 0:
            lp.update(wg=ffn_w_gate[j].astype(BF16), wu=ffn_w_up[j].astype(BF16),
                      wd=ffn_w_down[j].astype(BF16))
        else:
            wrT = jnp.transpose(moe_w_router[j]).astype(F32)
            wr_hi = wrT.astype(BF16)
            blocked = lambda w: jnp.transpose(
                w.astype(BF16).reshape(N_EXPERTS, W, -1, MOE_FFN_BLOCK), (0, 2, 1, 3))
            lp.update(wr_hi=wr_hi, wr_lo=(wrT - wr_hi.astype(F32)).astype(BF16),
                      wg=blocked(moe_w_gate[j]), wu=blocked(moe_w_up[j]),
                      wd=moe_w_down[j].astype(BF16))
        layers.append(lp)
    return layers, norm_final.reshape(1, W)


def _trunk(x3, layers, n_final):
    B, L, W = x3.shape
    x = x3.reshape(B * L, W)
    depth = len(layers)
    assert depth % 2 == 0, "the final RMSNorm is fused into the last (expert) layer"
    for i, lp in enumerate(layers):
        P, ab = _in_proj(x, lp["norm_mix"], lp["w_main"], lp["w_ab"])
        qn, kn, vc, knT, gcol, gT = _dn_prep(P, ab, lp["conv_w"], lp["neg_exp_alog"],
                                             lp["dt_bias"], L)
        o_f, o_b = _dn_scan(qn, kn, vc, knT, gcol, gT, L)
        x = _mix(P, o_f, o_b, x, lp["ln_g"], lp["ln_b"], lp["w_sp"], lp["b_spT"],
                 lp["o_norm"], lp["w_out"])
        if i % 2 == 0:
            x = _ffn(x, lp["norm_ffn"], lp["wg"], lp["wu"], lp["wd"])
        else:
            assert i == depth - 1
            x = _moe(x, lp["norm_ffn"], lp["wr_hi"], lp["wr_lo"], lp["wg"], lp["wu"], lp["wd"],
                     n_final)
    return x.reshape(B, L, W)


def kernel(x_prompt, x_sample, norm_mix, w_in, sgu_ln_gain, sgu_ln_bias, sgu_w_spatial, sgu_b_spatial, dn_conv_w, dn_a_log, dn_dt_bias, dn_out_norm, w_out, norm_ffn, ffn_w_gate, ffn_w_up, ffn_w_down, moe_w_router, moe_w_gate, moe_w_up, moe_w_down, norm_final):
    layers, n_final = _prepare(norm_mix, w_in, sgu_ln_gain, sgu_ln_bias, sgu_w_spatial,
                               sgu_b_spatial, dn_conv_w, dn_a_log, dn_dt_bias, dn_out_norm,
                               w_out, norm_ffn, ffn_w_gate, ffn_w_up, ffn_w_down, moe_w_router,
                               moe_w_gate, moe_w_up, moe_w_down, norm_final)
    return (_trunk(x_prompt, layers, n_final), _trunk(x_sample, layers, n_final))
```

```python
import functools

import jax
import jax.numpy as jnp
from jax import lax
from jax.experimental import pallas as pl
from jax.experimental.pallas import tpu as pltpu

F32 = jnp.float32
BF16 = jnp.bfloat16
EPS = 1e-6

D_MODEL = 1024
N_HEADS = 8
HEAD_DIM = 128
N_DIR = 2
CONV_K = 5
CONV_HALO = 8
HALO_BLOCK = 16
SGU_CHUNK = 128
SGU_GROUPS = 8
N_EXPERTS = 8
SCAN_CHUNK = 128
MOE_UNIT = 96
MOE_SMALL_BLOCK = 2 * MOE_UNIT
MOE_LARGE_BLOCK = 3 * MOE_UNIT
MOE_MOVE_BLOCK = 256
MOE_SUB_TILE = 1024
MOE_SUB_TILES = 2
MOE_FFN_BLOCK = 896
NEG_BIG = -1e30


def _params(sem, vmem_mb):
    return pltpu.CompilerParams(dimension_semantics=sem, vmem_limit_bytes=vmem_mb * 2**20)


def _dot(a, b):
    return jnp.dot(a.astype(BF16), b.astype(BF16), preferred_element_type=F32)


def _rms(xf, w):
    return xf * lax.rsqrt(jnp.mean(xf * xf, axis=-1, keepdims=True) + EPS) * w


def _silu(x):
    return x * jax.nn.sigmoid(x)


def _gelu(x):
    return 0.5 * x * (1.0 + lax.erf(x * (2.0 ** -0.5)))


def _in_proj_kernel(x_ref, nw_ref, w_ref, wab_ref, p_ref, ab_ref, h_ref):
    @pl.when(pl.program_id(1) == 0)
    def _():
        hb = _rms(x_ref[...], nw_ref[...]).astype(BF16)
        h_ref[...] = hb
        ab_ref[...] = jnp.dot(hb, wab_ref[...], preferred_element_type=F32)

    p_ref[...] = jnp.dot(h_ref[...], w_ref[...], preferred_element_type=F32).astype(BF16)


def _in_proj(x, nw, w_main, w_ab):
    T = x.shape[0]
    tm = min(1024, T)
    tn = 1024
    n_cols = w_main.shape[1]
    return pl.pallas_call(
        _in_proj_kernel,
        grid=(T // tm, n_cols // tn),
        in_specs=[
            pl.BlockSpec((tm, D_MODEL), lambda i, j: (i, 0)),
            pl.BlockSpec((1, D_MODEL), lambda i, j: (0, 0)),
            pl.BlockSpec((D_MODEL, tn), lambda i, j: (0, j)),
            pl.BlockSpec((D_MODEL, 128), lambda i, j: (0, 0)),
        ],
        out_specs=[
            pl.BlockSpec((tm, tn), lambda i, j: (i, j)),
            pl.BlockSpec((tm, 128), lambda i, j: (i, 0)),
        ],
        out_shape=[
            jax.ShapeDtypeStruct((T, n_cols), BF16),
            jax.ShapeDtypeStruct((T, 128), F32),
        ],
        scratch_shapes=[pltpu.VMEM((tm, D_MODEL), BF16)],
        compiler_params=_params(("parallel", "arbitrary"), 48),
        name="in_proj",
    )(x, nw, w_main, w_ab)


def _dn_prep_kernel(tiles_per_seq, pm_ref, pp_ref, pn_ref, ab_ref, cw_ref, nea_ref, dtb_ref,
                    qn_ref, kn_ref, vc_ref, knT_ref, gcol_ref, gT_ref, xp_ref):
    tl = pm_ref.shape[0]
    pos = pl.program_id(0) % tiles_per_seq
    keep_prev = jnp.where(pos == 0, 0.0, 1.0)
    keep_next = jnp.where(pos == tiles_per_seq - 1, 0.0, 1.0)
    q_scale = HEAD_DIM ** -0.5
    half = CONV_K // 2
    E = CONV_HALO
    ri = lax.broadcasted_iota(jnp.int32, (tl, tl), 0)
    ci = lax.broadcasted_iota(jnp.int32, (tl, tl), 1)
    offsets = [d for d in range(-half, half + 1) if d != 0]
    shift_mat = jnp.concatenate([jnp.where(ci == ri + d, 1.0, 0.0).astype(BF16) for d in offsets],
                                axis=0)

    for s in range(3):
        cs = slice(s * D_MODEL, (s + 1) * D_MODEL)
        xb = pm_ref[:, cs]
        xf = xb.astype(F32)
        taps = [cw_ref[k:k + 1, cs] for k in range(CONV_K)]
        shifted = jnp.dot(shift_mat, xb, preferred_element_type=F32)
        acc = None
        for k in range(CONV_K):
            d = k - half
            src = xf if d == 0 else shifted[offsets.index(d) * tl:(offsets.index(d) + 1) * tl]
            acc = src * taps[k] if acc is None else acc + src * taps[k]
        xp_ref[0, 0:E, :] = pp_ref[:, cs].astype(F32)[HALO_BLOCK - E:, :] * keep_prev
        xp_ref[0, E:3 * E, :] = xf[0:2 * E]
        xp_ref[1, 0:2 * E, :] = xf[tl - 2 * E:tl]
        xp_ref[1, 2 * E:3 * E, :] = pn_ref[:, cs].astype(F32)[:E, :] * keep_next
        edges = []
        for side, base in ((0, E), (1, E)):
            e_acc = None
            for k in range(CONV_K):
                start = base - half + k
                term = xp_ref[side, start:start + E, :] * taps[k]
                e_acc = term if e_acc is None else e_acc + term
            edges.append(e_acc)
        y = _silu(jnp.concatenate([edges[0], acc[E:tl - E], edges[1]], axis=0))
        if s == 2:
            vc_ref[...] = y.astype(BF16)
            continue
        for h in range(N_HEADS):
            hs = slice(h * HEAD_DIM, (h + 1) * HEAD_DIM)
            slab = y[:, hs]
            slab = slab * lax.rsqrt(jnp.sum(slab * slab, axis=-1, keepdims=True) + EPS)
            if s == 0:
                qn_ref[:, hs] = (slab * q_scale).astype(BF16)
            else:
                kn_ref[:, hs] = slab.astype(BF16)
                knT_ref[hs, :] = jnp.transpose(slab).astype(BF16)

    ab = ab_ref[...]
    z = ab + dtb_ref[...]
    softplus = jnp.maximum(z, 0.0) + jnp.log(1.0 + jnp.exp(-jnp.abs(z)))
    g = nea_ref[...] * softplus
    beta = jax.nn.sigmoid(ab)
    ri = lax.broadcasted_iota(jnp.int32, (tl, tl), 0)
    ci = lax.broadcasted_iota(jnp.int32, (tl, tl), 1)
    same = jnp.right_shift(ri, 7) == jnp.right_shift(ci, 7)
    lower = jnp.where(same & (ci <= ri), 1.0, 0.0).astype(BF16)
    upper = jnp.where(same & (ci >= ri), 1.0, 0.0).astype(BF16)
    g_hi = g.astype(BF16)
    g_lo = (g - g_hi.astype(F32)).astype(BF16)
    g_fwd = (jnp.dot(lower, g_hi, preferred_element_type=F32)
             + jnp.dot(lower, g_lo, preferred_element_type=F32))
    g_bwd = (jnp.dot(upper, g_hi, preferred_element_type=F32)
             + jnp.dot(upper, g_lo, preferred_element_type=F32))
    lane = lax.broadcasted_iota(jnp.int32, g.shape, 1)
    gcol = jnp.where(lane < N_HEADS, g_fwd, jnp.where(lane < N_DIR * N_HEADS, g_bwd, beta))
    gcol_ref[...] = gcol
    gT_ref[...] = jnp.transpose(gcol)[0:N_DIR * N_HEADS, :]


def _dn_prep(P, ab, conv_w, neg_exp_alog, dt_bias, seq_len):
    T = P.shape[0]
    tl = min(256, seq_len)
    tiles_per_seq = seq_len // tl
    hb = tl // HALO_BLOCK
    n_hb = T // HALO_BLOCK
    qkv_w = 3 * D_MODEL
    tok = jax.ShapeDtypeStruct((T, D_MODEL), BF16)
    return pl.pallas_call(
        functools.partial(_dn_prep_kernel, tiles_per_seq),
        grid=(T // tl,),
        in_specs=[
            pl.BlockSpec((tl, qkv_w), lambda i: (i, 0)),
            pl.BlockSpec((HALO_BLOCK, qkv_w), lambda i: (jnp.maximum(i * hb - 1, 0), 0)),
            pl.BlockSpec((HALO_BLOCK, qkv_w), lambda i: (jnp.minimum((i + 1) * hb, n_hb - 1), 0)),
            pl.BlockSpec((tl, 128), lambda i: (i, 0)),
            pl.BlockSpec((CONV_K, qkv_w), lambda i: (0, 0)),
            pl.BlockSpec((1, 128), lambda i: (0, 0)),
            pl.BlockSpec((1, 128), lambda i: (0, 0)),
        ],
        out_specs=[
            pl.BlockSpec((tl, D_MODEL), lambda i: (i, 0)),
            pl.BlockSpec((tl, D_MODEL), lambda i: (i, 0)),
            pl.BlockSpec((tl, D_MODEL), lambda i: (i, 0)),
            pl.BlockSpec((D_MODEL, tl), lambda i: (0, i)),
            pl.BlockSpec((tl, 128), lambda i: (i, 0)),
            pl.BlockSpec((N_DIR * N_HEADS, tl), lambda i: (0, i)),
        ],
        out_shape=[tok, tok, tok,
                   jax.ShapeDtypeStruct((D_MODEL, T), BF16),
                   jax.ShapeDtypeStruct((T, 128), F32),
                   jax.ShapeDtypeStruct((N_DIR * N_HEADS, T), F32)],
        scratch_shapes=[pltpu.VMEM((2, 3 * CONV_HALO, D_MODEL), F32)],
        compiler_params=_params(("parallel",), 48),
        name="dn_prep",
    )(P, P, P, ab, conv_w, neg_exp_alog, dt_bias)


def _stack(a, b):
    return jnp.concatenate([a.astype(BF16), b.astype(BF16)], axis=0)


def _merge_rows(direction, blk):
    first = blk if direction == 0 else 0
    return [slice(s, s + blk) for s in range(first, SCAN_CHUNK, 2 * blk)]


def _dn_scan_kernel(qf_ref, kf_ref, vf_ref, kTf_ref, gcf_ref, gTf_ref,
                    qb_ref, kb_ref, vb_ref, kTb_ref, gcb_ref, gTb_ref,
                    of_ref, ob_ref, S_ref):
    CH = SCAN_CHUNK
    refs = ((qf_ref, kf_ref, vf_ref, kTf_ref, gcf_ref, gTf_ref, of_ref),
            (qb_ref, kb_ref, vb_ref, kTb_ref, gcb_ref, gTb_ref, ob_ref))
    chains = [(d, h) for d in range(N_DIR) for h in range(N_HEADS)]
    C = range(len(chains))

    @pl.when(pl.program_id(1) == 0)
    def _():
        S_ref[...] = jnp.zeros_like(S_ref)

    ri = lax.broadcasted_iota(jnp.int32, (CH, CH), 0)
    ci = lax.broadcasted_iota(jnp.int32, (CH, CH), 1)
    incl = (ri >= ci, ri <= ci)
    strict = (ri > ci, ri < ci)
    end = (CH - 1, 0)
    eye = jnp.where(ri == ci, 1.0, 0.0)
    same = lambda sh: jnp.right_shift(ri, sh) == jnp.right_shift(ci, sh)
    m16, m32, m64 = same(4), same(5), same(6)
    levels = ((16, m32 & ~m16), (32, m64 & ~m32), (64, ~m64))
    hs = lambda h: slice(h * HEAD_DIM, (h + 1) * HEAD_DIM)
    gc = [r[4][...] for r in refs]
    gT = [r[5][...] for r in refs]

    def col(d, h, base):
        l = base + d * N_HEADS + h
        return gc[d][:, l:l + 1]

    Gc = [col(d, h, 0) for d, h in chains]
    beta = [col(d, h, N_DIR * N_HEADS) for d, h in chains]
    Gr = [gT[d][d * N_HEADS + h:d * N_HEADS + h + 1, :] for d, h in chains]
    G_end = [Gr[c][:, end[chains[c][0]]:end[chains[c][0]] + 1] for c in C]
    q = [refs[d][0][:, hs(h)] for d, h in chains]
    k = [refs[d][1][:, hs(h)] for d, h in chains]
    kT = [refs[d][3][hs(h), :] for d, h in chains]

    decay = [jnp.exp(jnp.where(incl[chains[c][0]], Gc[c] - Gr[c], NEG_BIG)) for c in C]
    KQ = [jnp.dot(jnp.concatenate([k[c], q[c]], axis=0), kT[c], preferred_element_type=F32) for c in C]
    A = [jnp.where(strict[chains[c][0]], beta[c] * KQ[c][:CH] * decay[c], 0.0) for c in C]
    qkm = [(KQ[c][CH:] * decay[c]).astype(BF16) for c in C]
    Dg = [jnp.where(m16, A[c], 0.0) for c in C]
    D2 = [_dot(Dg[c], Dg[c]) for c in C]
    X = [eye - Dg[c] for c in C]
    Y = [_dot(_stack(D2[c], X[c]), D2[c]) for c in C]
    D4 = [Y[c][:CH] for c in C]
    X = [X[c] + Y[c][CH:] for c in C]
    Y = [_dot(_stack(D4[c], X[c]), D4[c]) for c in C]
    X = [X[c] + Y[c][CH:] for c in C]
    X = [X[c] + _dot(X[c], Y[c][:CH]) for c in C]
    for blk, off in levels:
        rows = [_merge_rows(chains[c][0], blk) for c in C]
        Xr = [jnp.concatenate([X[c][r] for r in rows[c]], axis=0) for c in C]
        Z = [_dot(Xr[c], jnp.where(off, A[c], 0.0)) for c in C]
        Xr = [Xr[c] - _dot(Z[c], X[c]) for c in C]
        Xn = []
        for c in C:
            parts, pos = [], 0
            for j, r in enumerate(rows[c]):
                if r.start > pos:
                    parts.append(X[c][pos:r.start])
                parts.append(Xr[c][j * blk:(j + 1) * blk])
                pos = r.stop
            if pos < CH:
                parts.append(X[c][pos:CH])
            Xn.append(jnp.concatenate(parts, axis=0))
        X = Xn

    eG = [jnp.exp(Gc[c]) for c in C]
    S = [S_ref[c] for c in C]
    Sb = [S[c].astype(BF16) for c in C]
    kS = [jnp.dot((k[c].astype(F32) * (beta[c] * eG[c])).astype(BF16), Sb[c],
                  preferred_element_type=F32) for c in C]
    v_new = [_dot(X[c], refs[chains[c][0]][2][:, hs(chains[c][1])].astype(F32) * beta[c] - kS[c])
             for c in C]
    vnb = [v_new[c].astype(BF16) for c in C]
    o = [jnp.dot(jnp.concatenate([(q[c].astype(F32) * eG[c]).astype(BF16), qkm[c]], axis=1),
                 jnp.concatenate([Sb[c], vnb[c]], axis=0), preferred_element_type=F32) for c in C]
    dS = [jnp.dot((kT[c].astype(F32) * jnp.exp(G_end[c] - Gr[c])).astype(BF16), vnb[c],
                  preferred_element_type=F32) for c in C]
    for c in C:
        d, h = chains[c]
        refs[d][6][:, hs(h)] = o[c].astype(refs[d][6].dtype)
        S_ref[c] = S[c] * jnp.exp(G_end[c]) + dS[c]


def _dn_scan(qn, kn, vc, knT, gcol, gT, seq_len):
    T = qn.shape[0]
    CH = SCAN_CHUNK
    nc = seq_len // CH
    nb = T // seq_len
    fwd = lambda b, c: b * nc + c
    bwd = lambda b, c: b * nc + (nc - 1 - c)

    def specs(blk):
        tok = pl.BlockSpec((CH, D_MODEL), lambda b, c: (blk(b, c), 0))
        return [tok, tok, tok,
                pl.BlockSpec((D_MODEL, CH), lambda b, c: (0, blk(b, c))),
                pl.BlockSpec((CH, 128), lambda b, c: (blk(b, c), 0)),
                pl.BlockSpec((N_DIR * N_HEADS, CH), lambda b, c: (0, blk(b, c)))]

    out = jax.ShapeDtypeStruct((T, D_MODEL), BF16)
    args = (qn, kn, vc, knT, gcol, gT)
    return pl.pallas_call(
        _dn_scan_kernel,
        grid=(nb, nc),
        in_specs=specs(fwd) + specs(bwd),
        out_specs=[specs(fwd)[0], specs(bwd)[0]],
        out_shape=[out, out],
        scratch_shapes=[pltpu.VMEM((N_DIR * N_HEADS, HEAD_DIM, HEAD_DIM), F32)],
        compiler_params=_params(("parallel", "arbitrary"), 32),
        name="dn_scan",
    )(*args, *args)


def _mix_kernel(u_ref, vs_ref, z_ref, ga_ref, gb_ref, of_ref, ob_ref, x_ref,
                lng_ref, lnb_ref, ws_ref, bs_ref, on_ref, wo_ref, out_ref, mix_ref):
    tm = x_ref.shape[0]
    v = _gelu(vs_ref[...].astype(F32))
    vcen = v - jnp.mean(v, axis=-1, keepdims=True)
    vln = vcen * lax.rsqrt(jnp.mean(vcen * vcen, axis=-1, keepdims=True) + EPS)
    vb = (vln * lng_ref[...] + lnb_ref[...]).astype(BF16)
    for g in range(SGU_GROUPS):
        gs = slice(g * HEAD_DIM, (g + 1) * HEAD_DIM)
        u = _gelu(u_ref[:, gs].astype(F32))
        gate_a = jax.nn.sigmoid(ga_ref[:, gs].astype(F32))
        o = of_ref[:, gs].astype(F32) + ob_ref[:, gs].astype(F32)
        o = o * lax.rsqrt(jnp.mean(o * o, axis=-1, keepdims=True) + EPS) * on_ref[...]
        o = o * _silu(z_ref[:, gs].astype(F32))
        mixed = jax.nn.sigmoid(gb_ref[:, gs].astype(F32)) * o
        w_sp = ws_ref[g]
        bias = bs_ref[:, g:g + 1]
        for c in range(tm // SGU_CHUNK):
            rs = slice(c * SGU_CHUNK, (c + 1) * SGU_CHUNK)
            s = jnp.dot(w_sp, vb[rs, gs], preferred_element_type=F32) + bias
            mix_ref[rs, gs] = (mixed[rs] + gate_a[rs] * (u[rs] * s)).astype(BF16)
    out_ref[...] = x_ref[...] + jnp.dot(mix_ref[...], wo_ref[...], preferred_element_type=F32)


def _mix(P, o_f, o_b, x, ln_g, ln_b, w_sp, b_spT, o_norm, w_out):
    T = x.shape[0]
    tm = min(256, T)
    col = lambda j: pl.BlockSpec((tm, D_MODEL), lambda i: (i, j))
    tok = pl.BlockSpec((tm, D_MODEL), lambda i: (i, 0))
    vec = pl.BlockSpec((1, D_MODEL), lambda i: (0, 0))
    return pl.pallas_call(
        _mix_kernel,
        grid=(T // tm,),
        in_specs=[
            col(3), col(4), col(5), col(6), col(7), tok, tok, tok, vec, vec,
            pl.BlockSpec((SGU_GROUPS, SGU_CHUNK, SGU_CHUNK), lambda i: (0, 0, 0)),
            pl.BlockSpec((SGU_CHUNK, SGU_GROUPS), lambda i: (0, 0)),
            pl.BlockSpec((1, HEAD_DIM), lambda i: (0, 0)),
            pl.BlockSpec((D_MODEL, D_MODEL), lambda i: (0, 0)),
        ],
        out_specs=tok,
        out_shape=jax.ShapeDtypeStruct((T, D_MODEL), F32),
        scratch_shapes=[pltpu.VMEM((tm, D_MODEL), BF16)],
        compiler_params=_params(("parallel",), 48),
        name="mix",
    )(P, P, P, P, P, o_f, o_b, x, ln_g, ln_b, w_sp, b_spT, o_norm, w_out)


def _ffn_kernel(x_ref, nw_ref, wg_ref, wu_ref, wd_ref, o_ref, h_ref):
    @pl.when(pl.program_id(1) == 0)
    def _():
        xf = x_ref[...]
        h_ref[...] = _rms(xf, nw_ref[...]).astype(BF16)
        o_ref[...] = xf

    h = h_ref[...]
    g = jnp.dot(h, wg_ref[...], preferred_element_type=F32)
    u = jnp.dot(h, wu_ref[...], preferred_element_type=F32)
    o_ref[...] += jnp.dot((_silu(g) * u).astype(BF16), wd_ref[...], preferred_element_type=F32)


def _ffn(x, nw, wg, wu, wd):
    T = x.shape[0]
    F = wg.shape[1]
    tm = min(1024, T)
    tf = 256
    tok = pl.BlockSpec((tm, D_MODEL), lambda i, f: (i, 0))
    return pl.pallas_call(
        _ffn_kernel,
        grid=(T // tm, F // tf),
        in_specs=[
            tok,
            pl.BlockSpec((1, D_MODEL), lambda i, f: (0, 0)),
            pl.BlockSpec((D_MODEL, tf), lambda i, f: (0, f)),
            pl.BlockSpec((D_MODEL, tf), lambda i, f: (0, f)),
            pl.BlockSpec((tf, D_MODEL), lambda i, f: (f, 0)),
        ],
        out_specs=tok,
        out_shape=jax.ShapeDtypeStruct((T, D_MODEL), F32),
        scratch_shapes=[pltpu.VMEM((tm, D_MODEL), BF16)],
        compiler_params=_params(("parallel", "arbitrary"), 48),
        name="ffn",
    )(x, nw, wg, wu, wd)


def _moe_kernel(nf, ns, x_ref, nw_ref, wrh_ref, wrl_ref, wg_ref, wu_ref, wd_ref, nfin_ref, o_ref,
                h_ref, rank_ref, gate_ref, cnt_ref, xe_ref, ye_ref):
    e = pl.program_id(1)
    f = pl.program_id(2)
    Ts = x_ref.shape[0] // ns
    nt = (((1,), (1,)), ((), ()))
    tn = (((0,), (0,)), ((), ()))

    @pl.when((e == 0) & (f == 0))
    def _route():
        ri = lax.broadcasted_iota(jnp.int32, (Ts, Ts), 0)
        ci = lax.broadcasted_iota(jnp.int32, (Ts, Ts), 1)
        before = jnp.where(ri < ci, 1.0, 0.0).astype(BF16)
        wrh = wrh_ref[...]
        for s in range(ns):
            tok = slice(s * Ts, (s + 1) * Ts)
            xf = x_ref[tok, :]
            h = _rms(xf, nw_ref[...])
            hb = h.astype(BF16)
            h_ref[tok, :] = hb
            hl = (h - hb.astype(F32)).astype(BF16)
            logits = (lax.dot_general(wrh, hb, nt, preferred_element_type=F32)
                      + lax.dot_general(wrh, hl, nt, preferred_element_type=F32)
                      + lax.dot_general(wrl_ref[...], hb, nt, preferred_element_type=F32))
            ei = lax.broadcasted_iota(jnp.int32, logits.shape, 0)
            m1 = jnp.max(logits, axis=0, keepdims=True)
            i1 = jnp.min(jnp.where(logits == m1, ei, N_EXPERTS), axis=0, keepdims=True)
            s1 = ei == i1
            rest = jnp.where(s1, -jnp.inf, logits)
            m2 = jnp.max(rest, axis=0, keepdims=True)
            i2 = jnp.min(jnp.where(rest == m2, ei, N_EXPERTS), axis=0, keepdims=True)
            s2 = ei == i2
            e2 = jnp.exp(m2 - m1)
            w1 = 1.0 / (1.0 + e2)
            gate_ref[s] = jnp.where(s1, w1, 0.0) + jnp.where(s2, e2 * w1, 0.0)
            sel = jnp.where(s1 | s2, 1.0, 0.0)
            rank = jnp.dot(sel.astype(BF16), before, preferred_element_type=F32)
            rank_ref[s] = jnp.where(sel > 0.0, rank, -1.0)
            cnt = jnp.sum(sel, axis=1, keepdims=True)
            for ee in range(N_EXPERTS):
                cnt_ref[s * N_EXPERTS + ee] = cnt[ee, 0].astype(jnp.int32)
            o_ref[tok, :] = xf

    MB = MOE_MOVE_BLOCK
    for s in range(ns):
        tok = slice(s * Ts, (s + 1) * Ts)
        n_unit = (cnt_ref[s * N_EXPERTS + e] + MOE_UNIT - 1) // MOE_UNIT
        rem = n_unit % 3
        n_small = jnp.where(n_unit == 1, 1, jnp.where(rem == 0, 0, jnp.where(rem == 2, 1, 2)))
        n_large = (jnp.maximum(n_unit, 2) - 2 * n_small) // 3
        large_base = n_small * MOE_SMALL_BLOCK
        n_move = (large_base + n_large * MOE_LARGE_BLOCK + MB - 1) // MB
        rk = rank_ref[s, pl.ds(e, 1), :]

        def one_hot(m, value, rk=rk):
            rows = (lax.broadcasted_iota(jnp.int32, (MB, Ts), 0) + m * MB).astype(F32)
            return jnp.where(rk == rows, value, 0.0).astype(BF16)

        @pl.when(f == 0)
        def _gather(s=s, tok=tok, n_move=n_move, one_hot=one_hot):
            def body(m, carry):
                rows = pl.ds(pl.multiple_of(m * MB, MB), MB)
                xe_ref[s, rows, :] = jnp.dot(one_hot(m, 1.0), h_ref[tok, :],
                                             preferred_element_type=F32).astype(BF16)
                ye_ref[s, rows, :] = jnp.zeros((MB, D_MODEL), F32)
                return carry
            lax.fori_loop(0, n_move, body, 0)

        def expert_block(rows, s=s):
            xb = xe_ref[s, rows, :]
            g = jnp.dot(xb, wg_ref[0], preferred_element_type=F32)
            u = jnp.dot(xb, wu_ref[0], preferred_element_type=F32)
            ye_ref[s, rows, :] += jnp.dot((_silu(g) * u).astype(BF16), wd_ref[0],
                                          preferred_element_type=F32)

        def small_body(j, carry, expert_block=expert_block):
            expert_block(pl.ds(pl.multiple_of(j * MOE_SMALL_BLOCK, MOE_SMALL_BLOCK), MOE_SMALL_BLOCK))
            return carry
        lax.fori_loop(0, n_small, small_body, 0)

        def large_body(j, carry, expert_block=expert_block, large_base=large_base):
            expert_block(pl.ds(pl.multiple_of(large_base + j * MOE_LARGE_BLOCK, MOE_UNIT),
                               MOE_LARGE_BLOCK))
            return carry
        lax.fori_loop(0, n_large, large_body, 0)

        @pl.when(f == nf - 1)
        def _scatter(s=s, tok=tok, n_move=n_move, one_hot=one_hot):
            gt = gate_ref[s, pl.ds(e, 1), :]

            def body(m, carry):
                rows = pl.ds(pl.multiple_of(m * MB, MB), MB)
                o_ref[tok, :] += lax.dot_general(one_hot(m, gt), ye_ref[s, rows, :].astype(BF16),
                                                 tn, preferred_element_type=F32)
                return carry
            lax.fori_loop(0, n_move, body, 0)

    @pl.when((e == N_EXPERTS - 1) & (f == nf - 1))
    def _final():
        o_ref[...] = _rms(o_ref[...], nfin_ref[...])


def _moe(x, nw, wr_hi, wr_lo, wg, wu, wd, n_final):
    T = x.shape[0]
    tf = MOE_FFN_BLOCK
    nf = wg.shape[2] // tf
    ts = min(MOE_SUB_TILE, T)
    ns = min(MOE_SUB_TILES, T // ts)
    tm = ns * ts
    cap = -(-(ts + MOE_SMALL_BLOCK) // MOE_MOVE_BLOCK) * MOE_MOVE_BLOCK
    tok = pl.BlockSpec((tm, D_MODEL), lambda i, e, f: (i, 0), pipeline_mode=pl.Buffered(1))
    vec = pl.BlockSpec((1, D_MODEL), lambda i, e, f: (0, 0))
    wr = pl.BlockSpec((N_EXPERTS, D_MODEL), lambda i, e, f: (0, 0))
    w_in = pl.BlockSpec((1, D_MODEL, tf), lambda i, e, f: (e, 0, f))
    return pl.pallas_call(
        functools.partial(_moe_kernel, nf, ns),
        grid=(T // tm, N_EXPERTS, nf),
        in_specs=[
            tok, vec, wr, wr, w_in, w_in,
            pl.BlockSpec((1, tf, D_MODEL), lambda i, e, f: (e, f, 0)),
            vec,
        ],
        out_specs=tok,
        out_shape=jax.ShapeDtypeStruct((T, D_MODEL), F32),
        scratch_shapes=[
            pltpu.VMEM((tm, D_MODEL), BF16),
            pltpu.VMEM((ns, N_EXPERTS, ts), F32),
            pltpu.VMEM((ns, N_EXPERTS, ts), F32),
            pltpu.SMEM((ns * N_EXPERTS,), jnp.int32),
            pltpu.VMEM((ns, cap, D_MODEL), BF16),
            pltpu.VMEM((ns, cap, D_MODEL), F32),
        ],
        compiler_params=_params(("parallel", "arbitrary", "arbitrary"), 58),
        name="moe",
    )(x, nw, wr_hi, wr_lo, wg, wu, wd, n_final)


def _pad_lanes(v, width=128):
    v = v.reshape(1, -1).astype(F32)
    return jnp.pad(v, ((0, 0), (0, width - v.shape[1])))


def _prepare(norm_mix, w_in, sgu_ln_gain, sgu_ln_bias, sgu_w_spatial, sgu_b_spatial,
             dn_conv_w, dn_a_log, dn_dt_bias, dn_out_norm, w_out, norm_ffn,
             ffn_w_gate, ffn_w_up, ffn_w_down, moe_w_router, moe_w_gate, moe_w_up,
             moe_w_down, norm_final):
    depth = w_in.shape[0]
    W = D_MODEL
    layers = []
    for i in range(depth):
        wi = w_in[i]
        seg = lambda a, b: wi[:, a:b]
        ab0 = 6 * W
        ab1 = ab0 + 2 * N_DIR * N_HEADS
        w_main = jnp.concatenate(
            [seg(2 * W, 5 * W), seg(0, 2 * W), seg(5 * W, 6 * W), seg(ab1, ab1 + 2 * W)],
            axis=1).astype(BF16)
        w_ab = jnp.pad(seg(ab0, ab1), ((0, 0), (0, 128 - (ab1 - ab0)))).astype(BF16)
        lp = dict(
            norm_mix=norm_mix[i].reshape(1, W), w_main=w_main, w_ab=w_ab,
            ln_g=sgu_ln_gain[i].reshape(1, W), ln_b=sgu_ln_bias[i].reshape(1, W),
            w_sp=sgu_w_spatial[i].astype(BF16), b_spT=jnp.transpose(sgu_b_spatial[i]),
            conv_w=dn_conv_w[i],
            neg_exp_alog=_pad_lanes(-jnp.exp(dn_a_log[i].astype(F32))),
            dt_bias=_pad_lanes(dn_dt_bias[i]),
            o_norm=dn_out_norm[i].reshape(1, HEAD_DIM), w_out=w_out[i].astype(BF16),
            norm_ffn=norm_ffn[i].reshape(1, W),
        )
        j = i // 2
        if i % 2 == 0:
            lp.update(wg=ffn_w_gate[j].astype(BF16), wu=ffn_w_up[j].astype(BF16),
                      wd=ffn_w_down[j].astype(BF16))
        else:
            wrT = jnp.transpose(moe_w_router[j]).astype(F32)
            wr_hi = wrT.astype(BF16)
            lp.update(wr_hi=wr_hi, wr_lo=(wrT - wr_hi.astype(F32)).astype(BF16),
                      wg=moe_w_gate[j].astype(BF16), wu=moe_w_up[j].astype(BF16),
                      wd=moe_w_down[j].astype(BF16))
        layers.append(lp)
    return layers, norm_final.reshape(1, W)


def _trunk(x3, layers, n_final):
    B, L, W = x3.shape
    x = x3.reshape(B * L, W)
    depth = len(layers)
    assert depth % 2 == 0, "the final RMSNorm is fused into the last (expert) layer"
    for i, lp in enumerate(layers):
        P, ab = _in_proj(x, lp["norm_mix"], lp["w_main"], lp["w_ab"])
        qn, kn, vc, knT, gcol, gT = _dn_prep(P, ab, lp["conv_w"], lp["neg_exp_alog"],
                                             lp["dt_bias"], L)
        o_f, o_b = _dn_scan(qn, kn, vc, knT, gcol, gT, L)
        x = _mix(P, o_f, o_b, x, lp["ln_g"], lp["ln_b"], lp["w_sp"], lp["b_spT"],
                 lp["o_norm"], lp["w_out"])
        if i % 2 == 0:
            x = _ffn(x, lp["norm_ffn"], lp["wg"], lp["wu"], lp["wd"])
        else:
            assert i == depth - 1
            x = _moe(x, lp["norm_ffn"], lp["wr_hi"], lp["wr_lo"], lp["wg"], lp["wu"], lp["wd"],
                     n_final)
    return x.reshape(B, L, W)


def kernel(x_prompt, x_sample, norm_mix, w_in, sgu_ln_gain, sgu_ln_bias, sgu_w_spatial, sgu_b_spatial, dn_conv_w, dn_a_log, dn_dt_bias, dn_out_norm, w_out, norm_ffn, ffn_w_gate, ffn_w_up, ffn_w_down, moe_w_router, moe_w_gate, moe_w_up, moe_w_down, norm_final):
    layers, n_final = _prepare(norm_mix, w_in, sgu_ln_gain, sgu_ln_bias, sgu_w_spatial,
                               sgu_b_spatial, dn_conv_w, dn_a_log, dn_dt_bias, dn_out_norm,
                               w_out, norm_ffn, ffn_w_gate, ffn_w_up, ffn_w_down, moe_w_router,
                               moe_w_gate, moe_w_up, moe_w_down, norm_final)
    return (_trunk(x_prompt, layers, n_final), _trunk(x_sample, layers, n_final))
```

```python
import functools

import jax
import jax.numpy as jnp
from jax import lax
from jax.experimental import pallas as pl
from jax.experimental.pallas import tpu as pltpu

F32 = jnp.float32
BF16 = jnp.bfloat16
EPS = 1e-6

D_MODEL = 1024
N_HEADS = 8
HEAD_DIM = 128
N_DIR = 2
CONV_K = 5
CONV_HALO = 8
HALO_BLOCK = 16
SGU_CHUNK = 128
SGU_GROUPS = 8
N_EXPERTS = 8
SCAN_CHUNK = 128
MOE_UNIT = 96
MOE_SMALL_BLOCK = 2 * MOE_UNIT
MOE_LARGE_BLOCK = 3 * MOE_UNIT
MOE_MOVE_BLOCK = 256
MOE_SUB_TILE = 1024
MOE_SUB_TILES = 2
MOE_FFN_BLOCK = 896
NEG_BIG = -1e30


def _params(sem, vmem_mb):
    return pltpu.CompilerParams(dimension_semantics=sem, vmem_limit_bytes=vmem_mb * 2**20)


def _dot(a, b):
    return jnp.dot(a.astype(BF16), b.astype(BF16), preferred_element_type=F32)


def _rms(xf, w):
    return xf * lax.rsqrt(jnp.mean(xf * xf, axis=-1, keepdims=True) + EPS) * w


def _silu(x):
    return x * jax.nn.sigmoid(x)


def _gelu(x):
    return 0.5 * x * (1.0 + lax.erf(x * (2.0 ** -0.5)))


def _in_proj_kernel(x_ref, nw_ref, w_ref, wab_ref, p_ref, ab_ref):
    hb = _rms(x_ref[...], nw_ref[...]).astype(BF16)
    ab_ref[...] = jnp.dot(hb, wab_ref[...], preferred_element_type=F32)
    for j in range(w_ref.shape[1] // D_MODEL):
        cs = slice(j * D_MODEL, (j + 1) * D_MODEL)
        p_ref[:, cs] = jnp.dot(hb, w_ref[:, cs], preferred_element_type=F32).astype(BF16)


def _in_proj(x, nw, w_main, w_ab):
    T = x.shape[0]
    tm = min(512, T)
    n_cols = w_main.shape[1]
    resident = lambda shape: pl.BlockSpec(shape, lambda i: (0, 0), pipeline_mode=pl.Buffered(1))
    return pl.pallas_call(
        _in_proj_kernel,
        grid=(T // tm,),
        in_specs=[
            pl.BlockSpec((tm, D_MODEL), lambda i: (i, 0)),
            pl.BlockSpec((1, D_MODEL), lambda i: (0, 0)),
            resident((D_MODEL, n_cols)),
            resident((D_MODEL, 128)),
        ],
        out_specs=[
            pl.BlockSpec((tm, n_cols), lambda i: (i, 0)),
            pl.BlockSpec((tm, 128), lambda i: (i, 0)),
        ],
        out_shape=[
            jax.ShapeDtypeStruct((T, n_cols), BF16),
            jax.ShapeDtypeStruct((T, 128), F32),
        ],
        compiler_params=_params(("parallel",), 52),
        name="in_proj",
    )(x, nw, w_main, w_ab)


def _dn_prep_kernel(tiles_per_seq, pm_ref, pp_ref, pn_ref, ab_ref, cw_ref, nea_ref, dtb_ref,
                    qn_ref, kn_ref, vc_ref, knT_ref, gcol_ref, gT_ref, xp_ref):
    tl = pm_ref.shape[0]
    pos = pl.program_id(0) % tiles_per_seq
    keep_prev = jnp.where(pos == 0, 0.0, 1.0)
    keep_next = jnp.where(pos == tiles_per_seq - 1, 0.0, 1.0)
    q_scale = HEAD_DIM ** -0.5
    half = CONV_K // 2
    E = CONV_HALO
    ri = lax.broadcasted_iota(jnp.int32, (tl, tl), 0)
    ci = lax.broadcasted_iota(jnp.int32, (tl, tl), 1)
    offsets = [d for d in range(-half, half + 1) if d != 0]
    shift_mat = jnp.concatenate([jnp.where(ci == ri + d, 1.0, 0.0).astype(BF16) for d in offsets],
                                axis=0)

    for s in range(3):
        cs = slice(s * D_MODEL, (s + 1) * D_MODEL)
        xb = pm_ref[:, cs]
        xf = xb.astype(F32)
        taps = [cw_ref[k:k + 1, cs] for k in range(CONV_K)]
        shifted = jnp.dot(shift_mat, xb, preferred_element_type=F32)
        acc = None
        for k in range(CONV_K):
            d = k - half
            src = xf if d == 0 else shifted[offsets.index(d) * tl:(offsets.index(d) + 1) * tl]
            acc = src * taps[k] if acc is None else acc + src * taps[k]
        xp_ref[0, 0:E, :] = pp_ref[:, cs].astype(F32)[HALO_BLOCK - E:, :] * keep_prev
        xp_ref[0, E:3 * E, :] = xf[0:2 * E]
        xp_ref[1, 0:2 * E, :] = xf[tl - 2 * E:tl]
        xp_ref[1, 2 * E:3 * E, :] = pn_ref[:, cs].astype(F32)[:E, :] * keep_next
        edges = []
        for side, base in ((0, E), (1, E)):
            e_acc = None
            for k in range(CONV_K):
                start = base - half + k
                term = xp_ref[side, start:start + E, :] * taps[k]
                e_acc = term if e_acc is None else e_acc + term
            edges.append(e_acc)
        y = _silu(jnp.concatenate([edges[0], acc[E:tl - E], edges[1]], axis=0))
        if s == 2:
            vc_ref[...] = y.astype(BF16)
            continue
        for h in range(N_HEADS):
            hs = slice(h * HEAD_DIM, (h + 1) * HEAD_DIM)
            slab = y[:, hs]
            slab = slab * lax.rsqrt(jnp.sum(slab * slab, axis=-1, keepdims=True) + EPS)
            if s == 0:
                qn_ref[:, hs] = (slab * q_scale).astype(BF16)
            else:
                kn_ref[:, hs] = slab.astype(BF16)
                knT_ref[hs, :] = jnp.transpose(slab).astype(BF16)

    ab = ab_ref[...]
    z = ab + dtb_ref[...]
    softplus = jnp.maximum(z, 0.0) + jnp.log(1.0 + jnp.exp(-jnp.abs(z)))
    g = nea_ref[...] * softplus
    beta = jax.nn.sigmoid(ab)
    ri = lax.broadcasted_iota(jnp.int32, (tl, tl), 0)
    ci = lax.broadcasted_iota(jnp.int32, (tl, tl), 1)
    same = jnp.right_shift(ri, 7) == jnp.right_shift(ci, 7)
    lower = jnp.where(same & (ci <= ri), 1.0, 0.0).astype(BF16)
    upper = jnp.where(same & (ci >= ri), 1.0, 0.0).astype(BF16)
    g_hi = g.astype(BF16)
    g_lo = (g - g_hi.astype(F32)).astype(BF16)
    g_fwd = (jnp.dot(lower, g_hi, preferred_element_type=F32)
             + jnp.dot(lower, g_lo, preferred_element_type=F32))
    g_bwd = (jnp.dot(upper, g_hi, preferred_element_type=F32)
             + jnp.dot(upper, g_lo, preferred_element_type=F32))
    lane = lax.broadcasted_iota(jnp.int32, g.shape, 1)
    gcol = jnp.where(lane < N_HEADS, g_fwd, jnp.where(lane < N_DIR * N_HEADS, g_bwd, beta))
    gcol_ref[...] = gcol
    gT_ref[...] = jnp.transpose(gcol)[0:N_DIR * N_HEADS, :]


def _dn_prep(P, ab, conv_w, neg_exp_alog, dt_bias, seq_len):
    T = P.shape[0]
    tl = min(256, seq_len)
    tiles_per_seq = seq_len // tl
    hb = tl // HALO_BLOCK
    n_hb = T // HALO_BLOCK
    qkv_w = 3 * D_MODEL
    tok = jax.ShapeDtypeStruct((T, D_MODEL), BF16)
    return pl.pallas_call(
        functools.partial(_dn_prep_kernel, tiles_per_seq),
        grid=(T // tl,),
        in_specs=[
            pl.BlockSpec((tl, qkv_w), lambda i: (i, 0)),
            pl.BlockSpec((HALO_BLOCK, qkv_w), lambda i: (jnp.maximum(i * hb - 1, 0), 0)),
            pl.BlockSpec((HALO_BLOCK, qkv_w), lambda i: (jnp.minimum((i + 1) * hb, n_hb - 1), 0)),
            pl.BlockSpec((tl, 128), lambda i: (i, 0)),
            pl.BlockSpec((CONV_K, qkv_w), lambda i: (0, 0)),
            pl.BlockSpec((1, 128), lambda i: (0, 0)),
            pl.BlockSpec((1, 128), lambda i: (0, 0)),
        ],
        out_specs=[
            pl.BlockSpec((tl, D_MODEL), lambda i: (i, 0)),
            pl.BlockSpec((tl, D_MODEL), lambda i: (i, 0)),
            pl.BlockSpec((tl, D_MODEL), lambda i: (i, 0)),
            pl.BlockSpec((D_MODEL, tl), lambda i: (0, i)),
            pl.BlockSpec((tl, 128), lambda i: (i, 0)),
            pl.BlockSpec((N_DIR * N_HEADS, tl), lambda i: (0, i)),
        ],
        out_shape=[tok, tok, tok,
                   jax.ShapeDtypeStruct((D_MODEL, T), BF16),
                   jax.ShapeDtypeStruct((T, 128), F32),
                   jax.ShapeDtypeStruct((N_DIR * N_HEADS, T), F32)],
        scratch_shapes=[pltpu.VMEM((2, 3 * CONV_HALO, D_MODEL), F32)],
        compiler_params=_params(("parallel",), 48),
        name="dn_prep",
    )(P, P, P, ab, conv_w, neg_exp_alog, dt_bias)


def _stack(a, b):
    return jnp.concatenate([a.astype(BF16), b.astype(BF16)], axis=0)


def _merge_rows(direction, blk):
    first = blk if direction == 0 else 0
    return [slice(s, s + blk) for s in range(first, SCAN_CHUNK, 2 * blk)]


def _dn_scan_kernel(qf_ref, kf_ref, vf_ref, kTf_ref, gcf_ref, gTf_ref,
                    qb_ref, kb_ref, vb_ref, kTb_ref, gcb_ref, gTb_ref,
                    of_ref, ob_ref, S_ref):
    CH = SCAN_CHUNK
    refs = ((qf_ref, kf_ref, vf_ref, kTf_ref, gcf_ref, gTf_ref, of_ref),
            (qb_ref, kb_ref, vb_ref, kTb_ref, gcb_ref, gTb_ref, ob_ref))
    chains = [(d, h) for d in range(N_DIR) for h in range(N_HEADS)]
    C = range(len(chains))

    @pl.when(pl.program_id(1) == 0)
    def _():
        S_ref[...] = jnp.zeros_like(S_ref)

    ri = lax.broadcasted_iota(jnp.int32, (CH, CH), 0)
    ci = lax.broadcasted_iota(jnp.int32, (CH, CH), 1)
    incl = (ri >= ci, ri <= ci)
    strict = (ri > ci, ri < ci)
    end = (CH - 1, 0)
    eye = jnp.where(ri == ci, 1.0, 0.0)
    same = lambda sh: jnp.right_shift(ri, sh) == jnp.right_shift(ci, sh)
    m16, m32, m64 = same(4), same(5), same(6)
    levels = ((16, m32 & ~m16), (32, m64 & ~m32), (64, ~m64))
    hs = lambda h: slice(h * HEAD_DIM, (h + 1) * HEAD_DIM)
    gc = [r[4][...] for r in refs]
    gT = [r[5][...] for r in refs]

    def col(d, h, base):
        l = base + d * N_HEADS + h
        return gc[d][:, l:l + 1]

    Gc = [col(d, h, 0) for d, h in chains]
    beta = [col(d, h, N_DIR * N_HEADS) for d, h in chains]
    Gr = [gT[d][d * N_HEADS + h:d * N_HEADS + h + 1, :] for d, h in chains]
    G_end = [Gr[c][:, end[chains[c][0]]:end[chains[c][0]] + 1] for c in C]
    q = [refs[d][0][:, hs(h)] for d, h in chains]
    k = [refs[d][1][:, hs(h)] for d, h in chains]
    kT = [refs[d][3][hs(h), :] for d, h in chains]

    decay = [jnp.exp(jnp.where(incl[chains[c][0]], Gc[c] - Gr[c], NEG_BIG)) for c in C]
    KQ = [jnp.dot(jnp.concatenate([k[c], q[c]], axis=0), kT[c], preferred_element_type=F32) for c in C]
    A = [jnp.where(strict[chains[c][0]], beta[c] * KQ[c][:CH] * decay[c], 0.0) for c in C]
    qkm = [(KQ[c][CH:] * decay[c]).astype(BF16) for c in C]
    Dg = [jnp.where(m16, A[c], 0.0) for c in C]
    D2 = [_dot(Dg[c], Dg[c]) for c in C]
    X = [eye - Dg[c] for c in C]
    Y = [_dot(_stack(D2[c], X[c]), D2[c]) for c in C]
    D4 = [Y[c][:CH] for c in C]
    X = [X[c] + Y[c][CH:] for c in C]
    Y = [_dot(_stack(D4[c], X[c]), D4[c]) for c in C]
    X = [X[c] + Y[c][CH:] for c in C]
    X = [X[c] + _dot(X[c], Y[c][:CH]) for c in C]
    for blk, off in levels:
        rows = [_merge_rows(chains[c][0], blk) for c in C]
        Xr = [jnp.concatenate([X[c][r] for r in rows[c]], axis=0) for c in C]
        Z = [_dot(Xr[c], jnp.where(off, A[c], 0.0)) for c in C]
        Xr = [Xr[c] - _dot(Z[c], X[c]) for c in C]
        Xn = []
        for c in C:
            parts, pos = [], 0
            for j, r in enumerate(rows[c]):
                if r.start > pos:
                    parts.append(X[c][pos:r.start])
                parts.append(Xr[c][j * blk:(j + 1) * blk])
                pos = r.stop
            if pos < CH:
                parts.append(X[c][pos:CH])
            Xn.append(jnp.concatenate(parts, axis=0))
        X = Xn

    eG = [jnp.exp(Gc[c]) for c in C]
    S = [S_ref[c] for c in C]
    Sb = [S[c].astype(BF16) for c in C]
    kS = [jnp.dot((k[c].astype(F32) * (beta[c] * eG[c])).astype(BF16), Sb[c],
                  preferred_element_type=F32) for c in C]
    v_new = [_dot(X[c], refs[chains[c][0]][2][:, hs(chains[c][1])].astype(F32) * beta[c] - kS[c])
             for c in C]
    vnb = [v_new[c].astype(BF16) for c in C]
    o = [jnp.dot(jnp.concatenate([(q[c].astype(F32) * eG[c]).astype(BF16), qkm[c]], axis=1),
                 jnp.concatenate([Sb[c], vnb[c]], axis=0), preferred_element_type=F32) for c in C]
    dS = [jnp.dot((kT[c].astype(F32) * jnp.exp(G_end[c] - Gr[c])).astype(BF16), vnb[c],
                  preferred_element_type=F32) for c in C]
    for c in C:
        d, h = chains[c]
        refs[d][6][:, hs(h)] = o[c].astype(refs[d][6].dtype)
        S_ref[c] = S[c] * jnp.exp(G_end[c]) + dS[c]


def _dn_scan(qn, kn, vc, knT, gcol, gT, seq_len):
    T = qn.shape[0]
    CH = SCAN_CHUNK
    nc = seq_len // CH
    nb = T // seq_len
    fwd = lambda b, c: b * nc + c
    bwd = lambda b, c: b * nc + (nc - 1 - c)

    def specs(blk):
        tok = pl.BlockSpec((CH, D_MODEL), lambda b, c: (blk(b, c), 0))
        return [tok, tok, tok,
                pl.BlockSpec((D_MODEL, CH), lambda b, c: (0, blk(b, c))),
                pl.BlockSpec((CH, 128), lambda b, c: (blk(b, c), 0)),
                pl.BlockSpec((N_DIR * N_HEADS, CH), lambda b, c: (0, blk(b, c)))]

    out = jax.ShapeDtypeStruct((T, D_MODEL), BF16)
    args = (qn, kn, vc, knT, gcol, gT)
    return pl.pallas_call(
        _dn_scan_kernel,
        grid=(nb, nc),
        in_specs=specs(fwd) + specs(bwd),
        out_specs=[specs(fwd)[0], specs(bwd)[0]],
        out_shape=[out, out],
        scratch_shapes=[pltpu.VMEM((N_DIR * N_HEADS, HEAD_DIM, HEAD_DIM), F32)],
        compiler_params=_params(("parallel", "arbitrary"), 32),
        name="dn_scan",
    )(*args, *args)


def _mix_kernel(u_ref, vs_ref, z_ref, ga_ref, gb_ref, of_ref, ob_ref, x_ref,
                lng_ref, lnb_ref, ws_ref, bs_ref, on_ref, wo_ref, out_ref, mix_ref):
    tm = x_ref.shape[0]
    v = _gelu(vs_ref[...].astype(F32))
    vcen = v - jnp.mean(v, axis=-1, keepdims=True)
    vln = vcen * lax.rsqrt(jnp.mean(vcen * vcen, axis=-1, keepdims=True) + EPS)
    vb = (vln * lng_ref[...] + lnb_ref[...]).astype(BF16)
    for g in range(SGU_GROUPS):
        gs = slice(g * HEAD_DIM, (g + 1) * HEAD_DIM)
        u = _gelu(u_ref[:, gs].astype(F32))
        gate_a = jax.nn.sigmoid(ga_ref[:, gs].astype(F32))
        o = of_ref[:, gs].astype(F32) + ob_ref[:, gs].astype(F32)
        o = o * lax.rsqrt(jnp.mean(o * o, axis=-1, keepdims=True) + EPS) * on_ref[...]
        o = o * _silu(z_ref[:, gs].astype(F32))
        mixed = jax.nn.sigmoid(gb_ref[:, gs].astype(F32)) * o
        w_sp = ws_ref[g]
        bias = bs_ref[:, g:g + 1]
        for c in range(tm // SGU_CHUNK):
            rs = slice(c * SGU_CHUNK, (c + 1) * SGU_CHUNK)
            s = jnp.dot(w_sp, vb[rs, gs], preferred_element_type=F32) + bias
            mix_ref[rs, gs] = (mixed[rs] + gate_a[rs] * (u[rs] * s)).astype(BF16)
    out_ref[...] = x_ref[...] + jnp.dot(mix_ref[...], wo_ref[...], preferred_element_type=F32)


def _mix(P, o_f, o_b, x, ln_g, ln_b, w_sp, b_spT, o_norm, w_out):
    T = x.shape[0]
    tm = min(256, T)
    col = lambda j: pl.BlockSpec((tm, D_MODEL), lambda i: (i, j))
    tok = pl.BlockSpec((tm, D_MODEL), lambda i: (i, 0))
    vec = pl.BlockSpec((1, D_MODEL), lambda i: (0, 0))
    return pl.pallas_call(
        _mix_kernel,
        grid=(T // tm,),
        in_specs=[
            col(3), col(4), col(5), col(6), col(7), tok, tok, tok, vec, vec,
            pl.BlockSpec((SGU_GROUPS, SGU_CHUNK, SGU_CHUNK), lambda i: (0, 0, 0)),
            pl.BlockSpec((SGU_CHUNK, SGU_GROUPS), lambda i: (0, 0)),
            pl.BlockSpec((1, HEAD_DIM), lambda i: (0, 0)),
            pl.BlockSpec((D_MODEL, D_MODEL), lambda i: (0, 0)),
        ],
        out_specs=tok,
        out_shape=jax.ShapeDtypeStruct((T, D_MODEL), F32),
        scratch_shapes=[pltpu.VMEM((tm, D_MODEL), BF16)],
        compiler_params=_params(("parallel",), 48),
        name="mix",
    )(P, P, P, P, P, o_f, o_b, x, ln_g, ln_b, w_sp, b_spT, o_norm, w_out)


FFN_BLOCK = 256


def _ffn_kernel(x_ref, nw_ref, wg_ref, wu_ref, wd_ref, o_ref):
    xf = x_ref[...]
    h = _rms(xf, nw_ref[...]).astype(BF16)
    o_ref[...] = xf
    for f in range(wg_ref.shape[1] // FFN_BLOCK):
        fs = slice(f * FFN_BLOCK, (f + 1) * FFN_BLOCK)
        g = jnp.dot(h, wg_ref[:, fs], preferred_element_type=F32)
        u = jnp.dot(h, wu_ref[:, fs], preferred_element_type=F32)
        o_ref[...] += jnp.dot((_silu(g) * u).astype(BF16), wd_ref[fs, :], preferred_element_type=F32)


def _ffn(x, nw, wg, wu, wd):
    T = x.shape[0]
    F = wg.shape[1]
    tm = min(512, T)
    tok = pl.BlockSpec((tm, D_MODEL), lambda i: (i, 0))
    resident = lambda shape: pl.BlockSpec(shape, lambda i: (0, 0), pipeline_mode=pl.Buffered(1))
    return pl.pallas_call(
        _ffn_kernel,
        grid=(T // tm,),
        in_specs=[
            tok,
            pl.BlockSpec((1, D_MODEL), lambda i: (0, 0)),
            resident((D_MODEL, F)), resident((D_MODEL, F)), resident((F, D_MODEL)),
        ],
        out_specs=tok,
        out_shape=jax.ShapeDtypeStruct((T, D_MODEL), F32),
        compiler_params=_params(("parallel",), 52),
        name="ffn",
    )(x, nw, wg, wu, wd)


def _moe_kernel(nf, ns, x_ref, nw_ref, wrh_ref, wrl_ref, wg_ref, wu_ref, wd_ref, nfin_ref, o_ref,
                h_ref, rank_ref, gate_ref, cnt_ref, xe_ref, ye_ref):
    e = pl.program_id(1)
    f = pl.program_id(2)
    Ts = x_ref.shape[0] // ns
    nt = (((1,), (1,)), ((), ()))
    tn = (((0,), (0,)), ((), ()))

    @pl.when((e == 0) & (f == 0))
    def _route():
        ri = lax.broadcasted_iota(jnp.int32, (Ts, Ts), 0)
        ci = lax.broadcasted_iota(jnp.int32, (Ts, Ts), 1)
        before = jnp.where(ri < ci, 1.0, 0.0).astype(BF16)
        wrh = wrh_ref[...]
        for s in range(ns):
            tok = slice(s * Ts, (s + 1) * Ts)
            xf = x_ref[tok, :]
            h = _rms(xf, nw_ref[...])
            hb = h.astype(BF16)
            h_ref[tok, :] = hb
            hl = (h - hb.astype(F32)).astype(BF16)
            logits = (lax.dot_general(wrh, hb, nt, preferred_element_type=F32)
                      + lax.dot_general(wrh, hl, nt, preferred_element_type=F32)
                      + lax.dot_general(wrl_ref[...], hb, nt, preferred_element_type=F32))
            ei = lax.broadcasted_iota(jnp.int32, logits.shape, 0)
            m1 = jnp.max(logits, axis=0, keepdims=True)
            i1 = jnp.min(jnp.where(logits == m1, ei, N_EXPERTS), axis=0, keepdims=True)
            s1 = ei == i1
            rest = jnp.where(s1, -jnp.inf, logits)
            m2 = jnp.max(rest, axis=0, keepdims=True)
            i2 = jnp.min(jnp.where(rest == m2, ei, N_EXPERTS), axis=0, keepdims=True)
            s2 = ei == i2
            e2 = jnp.exp(m2 - m1)
            w1 = 1.0 / (1.0 + e2)
            gate_ref[s] = jnp.where(s1, w1, 0.0) + jnp.where(s2, e2 * w1, 0.0)
            sel = jnp.where(s1 | s2, 1.0, 0.0)
            rank = jnp.dot(sel.astype(BF16), before, preferred_element_type=F32)
            rank_ref[s] = jnp.where(sel > 0.0, rank, -1.0)
            cnt = jnp.sum(sel, axis=1, keepdims=True)
            for ee in range(N_EXPERTS):
                cnt_ref[s * N_EXPERTS + ee] = cnt[ee, 0].astype(jnp.int32)
            o_ref[tok, :] = xf

    MB = MOE_MOVE_BLOCK
    for s in range(ns):
        tok = slice(s * Ts, (s + 1) * Ts)
        n_unit = (cnt_ref[s * N_EXPERTS + e] + MOE_UNIT - 1) // MOE_UNIT
        rem = n_unit % 3
        n_small = jnp.where(n_unit == 1, 1, jnp.where(rem == 0, 0, jnp.where(rem == 2, 1, 2)))
        n_large = (jnp.maximum(n_unit, 2) - 2 * n_small) // 3
        large_base = n_small * MOE_SMALL_BLOCK
        n_move = (large_base + n_large * MOE_LARGE_BLOCK + MB - 1) // MB
        rk = rank_ref[s, pl.ds(e, 1), :]

        def one_hot(m, value, rk=rk):
            rows = (lax.broadcasted_iota(jnp.int32, (MB, Ts), 0) + m * MB).astype(F32)
            return jnp.where(rk == rows, value, 0.0).astype(BF16)

        @pl.when(f == 0)
        def _gather(s=s, tok=tok, n_move=n_move, one_hot=one_hot):
            def body(m, carry):
                rows = pl.ds(pl.multiple_of(m * MB, MB), MB)
                xe_ref[s, rows, :] = jnp.dot(one_hot(m, 1.0), h_ref[tok, :],
                                             preferred_element_type=F32).astype(BF16)
                ye_ref[s, rows, :] = jnp.zeros((MB, D_MODEL), F32)
                return carry
            lax.fori_loop(0, n_move, body, 0)

        def expert_block(rows, s=s):
            xb = xe_ref[s, rows, :]
            g = jnp.dot(xb, wg_ref[0], preferred_element_type=F32)
            u = jnp.dot(xb, wu_ref[0], preferred_element_type=F32)
            ye_ref[s, rows, :] += jnp.dot((_silu(g) * u).astype(BF16), wd_ref[0],
                                          preferred_element_type=F32)

        def small_body(j, carry, expert_block=expert_block):
            expert_block(pl.ds(pl.multiple_of(j * MOE_SMALL_BLOCK, MOE_SMALL_BLOCK), MOE_SMALL_BLOCK))
            return carry
        lax.fori_loop(0, n_small, small_body, 0)

        def large_body(j, carry, expert_block=expert_block, large_base=large_base):
            expert_block(pl.ds(pl.multiple_of(large_base + j * MOE_LARGE_BLOCK, MOE_UNIT),
                               MOE_LARGE_BLOCK))
            return carry
        lax.fori_loop(0, n_large, large_body, 0)

        @pl.when(f == nf - 1)
        def _scatter(s=s, tok=tok, n_move=n_move, one_hot=one_hot):
            gt = gate_ref[s, pl.ds(e, 1), :]

            def body(m, carry):
                rows = pl.ds(pl.multiple_of(m * MB, MB), MB)
                o_ref[tok, :] += lax.dot_general(one_hot(m, gt), ye_ref[s, rows, :].astype(BF16),
                                                 tn, preferred_element_type=F32)
                return carry
            lax.fori_loop(0, n_move, body, 0)

    @pl.when((e == N_EXPERTS - 1) & (f == nf - 1))
    def _final():
        o_ref[...] = _rms(o_ref[...], nfin_ref[...])


def _moe(x, nw, wr_hi, wr_lo, wg, wu, wd, n_final):
    T = x.shape[0]
    tf = MOE_FFN_BLOCK
    nf = wg.shape[2] // tf
    ts = min(MOE_SUB_TILE, T)
    ns = min(MOE_SUB_TILES, T // ts)
    tm = ns * ts
    cap = -(-(ts + MOE_SMALL_BLOCK) // MOE_MOVE_BLOCK) * MOE_MOVE_BLOCK
    tok = pl.BlockSpec((tm, D_MODEL), lambda i, e, f: (i, 0), pipeline_mode=pl.Buffered(1))
    vec = pl.BlockSpec((1, D_MODEL), lambda i, e, f: (0, 0))
    wr = pl.BlockSpec((N_EXPERTS, D_MODEL), lambda i, e, f: (0, 0))
    w_in = pl.BlockSpec((1, D_MODEL, tf), lambda i, e, f: (e, 0, f))
    return pl.pallas_call(
        functools.partial(_moe_kernel, nf, ns),
        grid=(T // tm, N_EXPERTS, nf),
        in_specs=[
            tok, vec, wr, wr, w_in, w_in,
            pl.BlockSpec((1, tf, D_MODEL), lambda i, e, f: (e, f, 0)),
            vec,
        ],
        out_specs=tok,
        out_shape=jax.ShapeDtypeStruct((T, D_MODEL), F32),
        scratch_shapes=[
            pltpu.VMEM((tm, D_MODEL), BF16),
            pltpu.VMEM((ns, N_EXPERTS, ts), F32),
            pltpu.VMEM((ns, N_EXPERTS, ts), F32),
            pltpu.SMEM((ns * N_EXPERTS,), jnp.int32),
            pltpu.VMEM((ns, cap, D_MODEL), BF16),
            pltpu.VMEM((ns, cap, D_MODEL), F32),
        ],
        compiler_params=_params(("parallel", "arbitrary", "arbitrary"), 58),
        name="moe",
    )(x, nw, wr_hi, wr_lo, wg, wu, wd, n_final)


def _pad_lanes(v, width=128):
    v = v.reshape(1, -1).astype(F32)
    return jnp.pad(v, ((0, 0), (0, width - v.shape[1])))


def _prepare(norm_mix, w_in, sgu_ln_gain, sgu_ln_bias, sgu_w_spatial, sgu_b_spatial,
             dn_conv_w, dn_a_log, dn_dt_bias, dn_out_norm, w_out, norm_ffn,
             ffn_w_gate, ffn_w_up, ffn_w_down, moe_w_router, moe_w_gate, moe_w_up,
             moe_w_down, norm_final):
    depth = w_in.shape[0]
    W = D_MODEL
    layers = []
    for i in range(depth):
        wi = w_in[i]
        seg = lambda a, b: wi[:, a:b]
        ab0 = 6 * W
        ab1 = ab0 + 2 * N_DIR * N_HEADS
        w_main = jnp.concatenate(
            [seg(2 * W, 5 * W), seg(0, 2 * W), seg(5 * W, 6 * W), seg(ab1, ab1 + 2 * W)],
            axis=1).astype(BF16)
        w_ab = jnp.pad(seg(ab0, ab1), ((0, 0), (0, 128 - (ab1 - ab0)))).astype(BF16)
        lp = dict(
            norm_mix=norm_mix[i].reshape(1, W), w_main=w_main, w_ab=w_ab,
            ln_g=sgu_ln_gain[i].reshape(1, W), ln_b=sgu_ln_bias[i].reshape(1, W),
            w_sp=sgu_w_spatial[i].astype(BF16), b_spT=jnp.transpose(sgu_b_spatial[i]),
            conv_w=dn_conv_w[i],
            neg_exp_alog=_pad_lanes(-jnp.exp(dn_a_log[i].astype(F32))),
            dt_bias=_pad_lanes(dn_dt_bias[i]),
            o_norm=dn_out_norm[i].reshape(1, HEAD_DIM), w_out=w_out[i].astype(BF16),
            norm_ffn=norm_ffn[i].reshape(1, W),
        )
        j = i // 2
        if i % 2 == 0:
            lp.update(wg=ffn_w_gate[j].astype(BF16), wu=ffn_w_up[j].astype(BF16),
                      wd=ffn_w_down[j].astype(BF16))
        else:
            wrT = jnp.transpose(moe_w_router[j]).astype(F32)
            wr_hi = wrT.astype(BF16)
            lp.update(wr_hi=wr_hi, wr_lo=(wrT - wr_hi.astype(F32)).astype(BF16),
                      wg=moe_w_gate[j].astype(BF16), wu=moe_w_up[j].astype(BF16),
                      wd=moe_w_down[j].astype(BF16))
        layers.append(lp)
    return layers, norm_final.reshape(1, W)


def _trunk(x3, layers, n_final):
    B, L, W = x3.shape
    x = x3.reshape(B * L, W)
    depth = len(layers)
    assert depth % 2 == 0, "the final RMSNorm is fused into the last (expert) layer"
    for i, lp in enumerate(layers):
        P, ab = _in_proj(x, lp["norm_mix"], lp["w_main"], lp["w_ab"])
        qn, kn, vc, knT, gcol, gT = _dn_prep(P, ab, lp["conv_w"], lp["neg_exp_alog"],
                                             lp["dt_bias"], L)
        o_f, o_b = _dn_scan(qn, kn, vc, knT, gcol, gT, L)
        x = _mix(P, o_f, o_b, x, lp["ln_g"], lp["ln_b"], lp["w_sp"], lp["b_spT"],
                 lp["o_norm"], lp["w_out"])
        if i % 2 == 0:
            x = _ffn(x, lp["norm_ffn"], lp["wg"], lp["wu"], lp["wd"])
        else:
            assert i == depth - 1
            x = _moe(x, lp["norm_ffn"], lp["wr_hi"], lp["wr_lo"], lp["wg"], lp["wu"], lp["wd"],
                     n_final)
    return x.reshape(B, L, W)


def kernel(x_prompt, x_sample, norm_mix, w_in, sgu_ln_gain, sgu_ln_bias, sgu_w_spatial, sgu_b_spatial, dn_conv_w, dn_a_log, dn_dt_bias, dn_out_norm, w_out, norm_ffn, ffn_w_gate, ffn_w_up, ffn_w_down, moe_w_router, moe_w_gate, moe_w_up, moe_w_down, norm_final):
    layers, n_final = _prepare(norm_mix, w_in, sgu_ln_gain, sgu_ln_bias, sgu_w_spatial,
                               sgu_b_spatial, dn_conv_w, dn_a_log, dn_dt_bias, dn_out_norm,
                               w_out, norm_ffn, ffn_w_gate, ffn_w_up, ffn_w_down, moe_w_router,
                               moe_w_gate, moe_w_up, moe_w_down, norm_final)
    return (_trunk(x_prompt, layers, n_final), _trunk(x_sample, layers, n_final))
```

```python
import functools

import jax
import jax.numpy as jnp
from jax import lax
from jax.experimental import pallas as pl
from jax.experimental.pallas import tpu as pltpu

F32 = jnp.float32
BF16 = jnp.bfloat16
EPS = 1e-6

D_MODEL = 1024
N_HEADS = 8
HEAD_DIM = 128
N_DIR = 2
CONV_K = 5
CONV_HALO = 8
SGU_CHUNK = 128
SGU_GROUPS = 8
N_EXPERTS = 8
SCAN_CHUNK = 128
MOE_UNIT = 96
MOE_SMALL_BLOCK = 2 * MOE_UNIT
MOE_LARGE_BLOCK = 3 * MOE_UNIT
MOE_SUB_TILE = 1024
MOE_SUB_TILES = 2
MOE_FFN_BLOCK = 896
NEG_BIG = -1e30


def _params(sem, vmem_mb):
    return pltpu.CompilerParams(dimension_semantics=sem, vmem_limit_bytes=vmem_mb * 2**20)


def _dot(a, b):
    return jnp.dot(a.astype(BF16), b.astype(BF16), preferred_element_type=F32)


def _rms(xf, w):
    return xf * lax.rsqrt(jnp.mean(xf * xf, axis=-1, keepdims=True) + EPS) * w


def _silu(x):
    return x * jax.nn.sigmoid(x)


def _gelu(x):
    return 0.5 * x * (1.0 + lax.erf(x * (2.0 ** -0.5)))


PROJ_TILE = 512


def _proj_prep_kernel(tiles_per_seq, x_ref, xp_ref, xn_ref, nw_ref, w_ref, wab_ref, cw_ref, nea_ref,
                      dtb_ref, p_ref, qn_ref, kn_ref, vc_ref, knT_ref, gcol_ref, gT_ref, buf_ref):
    tm = x_ref.shape[0]
    E = CONV_HALO
    half = CONV_K // 2
    pos = pl.program_id(0) % tiles_per_seq
    keep_prev = jnp.where(pos == 0, 0.0, 1.0)
    keep_next = jnp.where(pos == tiles_per_seq - 1, 0.0, 1.0)
    q_scale = HEAD_DIM ** -0.5
    nw = nw_ref[...]
    hb = _rms(x_ref[...], nw).astype(BF16)
    h_halo = _rms(jnp.concatenate([xp_ref[...], xn_ref[...]], axis=0), nw).astype(BF16)

    rest = iter(range(p_ref.shape[1] // D_MODEL))

    def project_rest(count):
        for _ in range(count):
            j = next(rest, None)
            if j is not None:
                cs = slice((3 + j) * D_MODEL, (4 + j) * D_MODEL)
                p_ref[:, j * D_MODEL:(j + 1) * D_MODEL] = jnp.dot(
                    hb, w_ref[:, cs], preferred_element_type=F32).astype(BF16)

    for s in range(3):
        cs = slice(s * D_MODEL, (s + 1) * D_MODEL)
        proj = jnp.dot(hb, w_ref[:, cs], preferred_element_type=F32).astype(BF16).astype(F32)
        halo = jnp.dot(h_halo, w_ref[:, cs], preferred_element_type=F32).astype(BF16).astype(F32)
        project_rest(2)
        buf_ref[0:E, :] = halo[0:E] * keep_prev
        buf_ref[E:E + tm, :] = proj
        buf_ref[E + tm:2 * E + tm, :] = halo[E:2 * E] * keep_next
        acc = None
        for k in range(CONV_K):
            start = E - half + k
            term = buf_ref[start:start + tm, :] * cw_ref[k:k + 1, cs]
            acc = term if acc is None else acc + term
        y = _silu(acc)
        if s == 2:
            vc_ref[...] = y.astype(BF16)
            continue
        for h in range(N_HEADS):
            hs = slice(h * HEAD_DIM, (h + 1) * HEAD_DIM)
            slab = y[:, hs]
            slab = slab * lax.rsqrt(jnp.sum(slab * slab, axis=-1, keepdims=True) + EPS)
            if s == 0:
                qn_ref[:, hs] = (slab * q_scale).astype(BF16)
            else:
                kn_ref[:, hs] = slab.astype(BF16)
                knT_ref[hs, :] = jnp.transpose(slab).astype(BF16)
    project_rest(p_ref.shape[1] // D_MODEL)

    ab = jnp.dot(hb, wab_ref[...], preferred_element_type=F32)
    z = ab + dtb_ref[...]
    softplus = jnp.maximum(z, 0.0) + jnp.log(1.0 + jnp.exp(-jnp.abs(z)))
    g = nea_ref[...] * softplus
    beta = jax.nn.sigmoid(ab)
    ri = lax.broadcasted_iota(jnp.int32, (tm, tm), 0)
    ci = lax.broadcasted_iota(jnp.int32, (tm, tm), 1)
    same = jnp.right_shift(ri, 7) == jnp.right_shift(ci, 7)
    lower = jnp.where(same & (ci <= ri), 1.0, 0.0).astype(BF16)
    upper = jnp.where(same & (ci >= ri), 1.0, 0.0).astype(BF16)
    g_hi = g.astype(BF16)
    g_lo = (g - g_hi.astype(F32)).astype(BF16)
    g_fwd = (jnp.dot(lower, g_hi, preferred_element_type=F32)
             + jnp.dot(lower, g_lo, preferred_element_type=F32))
    g_bwd = (jnp.dot(upper, g_hi, preferred_element_type=F32)
             + jnp.dot(upper, g_lo, preferred_element_type=F32))
    lane = lax.broadcasted_iota(jnp.int32, g.shape, 1)
    gcol = jnp.where(lane < N_HEADS, g_fwd, jnp.where(lane < N_DIR * N_HEADS, g_bwd, beta))
    gcol_ref[...] = gcol
    gT_ref[...] = jnp.transpose(gcol)[0:N_DIR * N_HEADS, :]


def _proj_prep(x, nw, w_main, w_ab, conv_w, neg_exp_alog, dt_bias, seq_len):
    T = x.shape[0]
    tm = min(PROJ_TILE, seq_len)
    tiles_per_seq = seq_len // tm
    hb = tm // CONV_HALO
    n_hb = T // CONV_HALO
    n_cols = w_main.shape[1]
    rest = n_cols - 3 * D_MODEL
    resident = lambda shape: pl.BlockSpec(shape, lambda i: (0, 0), pipeline_mode=pl.Buffered(1))
    vec = lambda n: pl.BlockSpec((1, n), lambda i: (0, 0))
    tok = jax.ShapeDtypeStruct((T, D_MODEL), BF16)
    tok_spec = pl.BlockSpec((tm, D_MODEL), lambda i: (i, 0))
    return pl.pallas_call(
        functools.partial(_proj_prep_kernel, tiles_per_seq),
        grid=(T // tm,),
        in_specs=[
            tok_spec,
            pl.BlockSpec((CONV_HALO, D_MODEL), lambda i: (jnp.maximum(i * hb - 1, 0), 0)),
            pl.BlockSpec((CONV_HALO, D_MODEL), lambda i: (jnp.minimum((i + 1) * hb, n_hb - 1), 0)),
            vec(D_MODEL),
            resident((D_MODEL, n_cols)),
            resident((D_MODEL, 128)),
            pl.BlockSpec((CONV_K, 3 * D_MODEL), lambda i: (0, 0)),
            vec(128), vec(128),
        ],
        out_specs=[
            pl.BlockSpec((tm, rest), lambda i: (i, 0)),
            tok_spec, tok_spec, tok_spec,
            pl.BlockSpec((D_MODEL, tm), lambda i: (0, i)),
            pl.BlockSpec((tm, 128), lambda i: (i, 0)),
            pl.BlockSpec((N_DIR * N_HEADS, tm), lambda i: (0, i)),
        ],
        out_shape=[jax.ShapeDtypeStruct((T, rest), BF16), tok, tok, tok,
                   jax.ShapeDtypeStruct((D_MODEL, T), BF16),
                   jax.ShapeDtypeStruct((T, 128), F32),
                   jax.ShapeDtypeStruct((N_DIR * N_HEADS, T), F32)],
        scratch_shapes=[pltpu.VMEM((tm + 2 * CONV_HALO, D_MODEL), F32)],
        compiler_params=_params(("parallel",), 56),
        name="proj_prep",
    )(x, x, x, nw, w_main, w_ab, conv_w, neg_exp_alog, dt_bias)


def _stack(a, b):
    return jnp.concatenate([a.astype(BF16), b.astype(BF16)], axis=0)


def _merge_rows(direction, blk):
    first = blk if direction == 0 else 0
    return [slice(s, s + blk) for s in range(first, SCAN_CHUNK, 2 * blk)]


def _dn_scan_kernel(qf_ref, kf_ref, vf_ref, kTf_ref, gcf_ref, gTf_ref,
                    qb_ref, kb_ref, vb_ref, kTb_ref, gcb_ref, gTb_ref,
                    of_ref, ob_ref, S_ref):
    CH = SCAN_CHUNK
    refs = ((qf_ref, kf_ref, vf_ref, kTf_ref, gcf_ref, gTf_ref, of_ref),
            (qb_ref, kb_ref, vb_ref, kTb_ref, gcb_ref, gTb_ref, ob_ref))
    chains = [(d, h) for d in range(N_DIR) for h in range(N_HEADS)]
    C = range(len(chains))

    @pl.when(pl.program_id(1) == 0)
    def _():
        S_ref[...] = jnp.zeros_like(S_ref)

    ri = lax.broadcasted_iota(jnp.int32, (CH, CH), 0)
    ci = lax.broadcasted_iota(jnp.int32, (CH, CH), 1)
    incl = (ri >= ci, ri <= ci)
    strict = (ri > ci, ri < ci)
    end = (CH - 1, 0)
    eye = jnp.where(ri == ci, 1.0, 0.0)
    same = lambda sh: jnp.right_shift(ri, sh) == jnp.right_shift(ci, sh)
    m16, m32, m64 = same(4), same(5), same(6)
    levels = ((16, m32 & ~m16), (32, m64 & ~m32), (64, ~m64))
    hs = lambda h: slice(h * HEAD_DIM, (h + 1) * HEAD_DIM)
    gc = [r[4][...] for r in refs]
    gT = [r[5][...] for r in refs]

    def col(d, h, base):
        l = base + d * N_HEADS + h
        return gc[d][:, l:l + 1]

    Gc = [col(d, h, 0) for d, h in chains]
    beta = [col(d, h, N_DIR * N_HEADS) for d, h in chains]
    Gr = [gT[d][d * N_HEADS + h:d * N_HEADS + h + 1, :] for d, h in chains]
    G_end = [Gr[c][:, end[chains[c][0]]:end[chains[c][0]] + 1] for c in C]
    q = [refs[d][0][:, hs(h)] for d, h in chains]
    k = [refs[d][1][:, hs(h)] for d, h in chains]
    kT = [refs[d][3][hs(h), :] for d, h in chains]

    decay = [jnp.exp(jnp.where(incl[chains[c][0]], Gc[c] - Gr[c], NEG_BIG)) for c in C]
    KQ = [jnp.dot(jnp.concatenate([k[c], q[c]], axis=0), kT[c], preferred_element_type=F32) for c in C]
    A = [jnp.where(strict[chains[c][0]], beta[c] * KQ[c][:CH] * decay[c], 0.0) for c in C]
    qkm = [(KQ[c][CH:] * decay[c]).astype(BF16) for c in C]
    Dg = [jnp.where(m16, A[c], 0.0) for c in C]
    D2 = [_dot(Dg[c], Dg[c]) for c in C]
    X = [eye - Dg[c] for c in C]
    Y = [_dot(_stack(D2[c], X[c]), D2[c]) for c in C]
    D4 = [Y[c][:CH] for c in C]
    X = [X[c] + Y[c][CH:] for c in C]
    Y = [_dot(_stack(D4[c], X[c]), D4[c]) for c in C]
    X = [X[c] + Y[c][CH:] for c in C]
    X = [X[c] + _dot(X[c], Y[c][:CH]) for c in C]
    for blk, off in levels:
        rows = [_merge_rows(chains[c][0], blk) for c in C]
        Xr = [jnp.concatenate([X[c][r] for r in rows[c]], axis=0) for c in C]
        Z = [_dot(Xr[c], jnp.where(off, A[c], 0.0)) for c in C]
        Xr = [Xr[c] - _dot(Z[c], X[c]) for c in C]
        Xn = []
        for c in C:
            parts, pos = [], 0
            for j, r in enumerate(rows[c]):
                if r.start > pos:
                    parts.append(X[c][pos:r.start])
                parts.append(Xr[c][j * blk:(j + 1) * blk])
                pos = r.stop
            if pos < CH:
                parts.append(X[c][pos:CH])
            Xn.append(jnp.concatenate(parts, axis=0))
        X = Xn

    eG = [jnp.exp(Gc[c]) for c in C]
    S = [S_ref[c] for c in C]
    Sb = [S[c].astype(BF16) for c in C]
    kS = [jnp.dot((k[c].astype(F32) * (beta[c] * eG[c])).astype(BF16), Sb[c],
                  preferred_element_type=F32) for c in C]
    v_new = [_dot(X[c], refs[chains[c][0]][2][:, hs(chains[c][1])].astype(F32) * beta[c] - kS[c])
             for c in C]
    vnb = [v_new[c].astype(BF16) for c in C]
    o = [jnp.dot(jnp.concatenate([(q[c].astype(F32) * eG[c]).astype(BF16), qkm[c]], axis=1),
                 jnp.concatenate([Sb[c], vnb[c]], axis=0), preferred_element_type=F32) for c in C]
    dS = [jnp.dot((kT[c].astype(F32) * jnp.exp(G_end[c] - Gr[c])).astype(BF16), vnb[c],
                  preferred_element_type=F32) for c in C]
    for c in C:
        d, h = chains[c]
        refs[d][6][:, hs(h)] = o[c].astype(refs[d][6].dtype)
        S_ref[c] = S[c] * jnp.exp(G_end[c]) + dS[c]


def _dn_scan(qn, kn, vc, knT, gcol, gT, seq_len):
    T = qn.shape[0]
    CH = SCAN_CHUNK
    nc = seq_len // CH
    nb = T // seq_len
    fwd = lambda b, c: b * nc + c
    bwd = lambda b, c: b * nc + (nc - 1 - c)

    def specs(blk):
        tok = pl.BlockSpec((CH, D_MODEL), lambda b, c: (blk(b, c), 0))
        return [tok, tok, tok,
                pl.BlockSpec((D_MODEL, CH), lambda b, c: (0, blk(b, c))),
                pl.BlockSpec((CH, 128), lambda b, c: (blk(b, c), 0)),
                pl.BlockSpec((N_DIR * N_HEADS, CH), lambda b, c: (0, blk(b, c)))]

    out = jax.ShapeDtypeStruct((T, D_MODEL), BF16)
    args = (qn, kn, vc, knT, gcol, gT)
    return pl.pallas_call(
        _dn_scan_kernel,
        grid=(nb, nc),
        in_specs=specs(fwd) + specs(bwd),
        out_specs=[specs(fwd)[0], specs(bwd)[0]],
        out_shape=[out, out],
        scratch_shapes=[pltpu.VMEM((N_DIR * N_HEADS, HEAD_DIM, HEAD_DIM), F32)],
        compiler_params=_params(("parallel", "arbitrary"), 32),
        name="dn_scan",
    )(*args, *args)


def _mix_kernel(u_ref, vs_ref, z_ref, ga_ref, gb_ref, of_ref, ob_ref, x_ref,
                lng_ref, lnb_ref, ws_ref, bs_ref, on_ref, wo_ref, out_ref, mix_ref):
    tm = x_ref.shape[0]
    v = _gelu(vs_ref[...].astype(F32))
    vcen = v - jnp.mean(v, axis=-1, keepdims=True)
    vln = vcen * lax.rsqrt(jnp.mean(vcen * vcen, axis=-1, keepdims=True) + EPS)
    vb = (vln * lng_ref[...] + lnb_ref[...]).astype(BF16)
    for g in range(SGU_GROUPS):
        gs = slice(g * HEAD_DIM, (g + 1) * HEAD_DIM)
        u = _gelu(u_ref[:, gs].astype(F32))
        gate_a = jax.nn.sigmoid(ga_ref[:, gs].astype(F32))
        o = of_ref[:, gs].astype(F32) + ob_ref[:, gs].astype(F32)
        o = o * lax.rsqrt(jnp.mean(o * o, axis=-1, keepdims=True) + EPS) * on_ref[...]
        o = o * _silu(z_ref[:, gs].astype(F32))
        mixed = jax.nn.sigmoid(gb_ref[:, gs].astype(F32)) * o
        w_sp = ws_ref[g]
        bias = bs_ref[:, g:g + 1]
        for c in range(tm // SGU_CHUNK):
            rs = slice(c * SGU_CHUNK, (c + 1) * SGU_CHUNK)
            s = jnp.dot(w_sp, vb[rs, gs], preferred_element_type=F32) + bias
            mix_ref[rs, gs] = (mixed[rs] + gate_a[rs] * (u[rs] * s)).astype(BF16)
    out_ref[...] = x_ref[...] + jnp.dot(mix_ref[...], wo_ref[...], preferred_element_type=F32)


def _mix(P, o_f, o_b, x, ln_g, ln_b, w_sp, b_spT, o_norm, w_out):
    T = x.shape[0]
    tm = min(256, T)
    col = lambda j: pl.BlockSpec((tm, D_MODEL), lambda i: (i, j))
    tok = pl.BlockSpec((tm, D_MODEL), lambda i: (i, 0))
    vec = pl.BlockSpec((1, D_MODEL), lambda i: (0, 0))
    return pl.pallas_call(
        _mix_kernel,
        grid=(T // tm,),
        in_specs=[
            col(0), col(1), col(2), col(3), col(4), tok, tok, tok, vec, vec,
            pl.BlockSpec((SGU_GROUPS, SGU_CHUNK, SGU_CHUNK), lambda i: (0, 0, 0)),
            pl.BlockSpec((SGU_CHUNK, SGU_GROUPS), lambda i: (0, 0)),
            pl.BlockSpec((1, HEAD_DIM), lambda i: (0, 0)),
            pl.BlockSpec((D_MODEL, D_MODEL), lambda i: (0, 0)),
        ],
        out_specs=tok,
        out_shape=jax.ShapeDtypeStruct((T, D_MODEL), F32),
        scratch_shapes=[pltpu.VMEM((tm, D_MODEL), BF16)],
        compiler_params=_params(("parallel",), 48),
        name="mix",
    )(P, P, P, P, P, o_f, o_b, x, ln_g, ln_b, w_sp, b_spT, o_norm, w_out)


FFN_BLOCK = 256


def _ffn_kernel(x_ref, nw_ref, wg_ref, wu_ref, wd_ref, o_ref):
    xf = x_ref[...]
    h = _rms(xf, nw_ref[...]).astype(BF16)
    o_ref[...] = xf
    for f in range(wg_ref.shape[1] // FFN_BLOCK):
        fs = slice(f * FFN_BLOCK, (f + 1) * FFN_BLOCK)
        g = jnp.dot(h, wg_ref[:, fs], preferred_element_type=F32)
        u = jnp.dot(h, wu_ref[:, fs], preferred_element_type=F32)
        o_ref[...] += jnp.dot((_silu(g) * u).astype(BF16), wd_ref[fs, :], preferred_element_type=F32)


def _ffn(x, nw, wg, wu, wd):
    T = x.shape[0]
    F = wg.shape[1]
    tm = min(512, T)
    tok = pl.BlockSpec((tm, D_MODEL), lambda i: (i, 0))
    resident = lambda shape: pl.BlockSpec(shape, lambda i: (0, 0), pipeline_mode=pl.Buffered(1))
    return pl.pallas_call(
        _ffn_kernel,
        grid=(T // tm,),
        in_specs=[
            tok,
            pl.BlockSpec((1, D_MODEL), lambda i: (0, 0)),
            resident((D_MODEL, F)), resident((D_MODEL, F)), resident((F, D_MODEL)),
        ],
        out_specs=tok,
        out_shape=jax.ShapeDtypeStruct((T, D_MODEL), F32),
        compiler_params=_params(("parallel",), 52),
        name="ffn",
    )(x, nw, wg, wu, wd)


def _moe_kernel(nf, ns, x_ref, nw_ref, wrh_ref, wrl_ref, wg_ref, wu_ref, wd_ref, nfin_ref, o_ref,
                h_ref, rank_ref, gate_ref, cnt_ref, xe_ref, ye_ref):
    e = pl.program_id(1)
    f = pl.program_id(2)
    Ts = x_ref.shape[0] // ns
    nt = (((1,), (1,)), ((), ()))
    tn = (((0,), (0,)), ((), ()))

    @pl.when((e == 0) & (f == 0))
    def _route():
        ri = lax.broadcasted_iota(jnp.int32, (Ts, Ts), 0)
        ci = lax.broadcasted_iota(jnp.int32, (Ts, Ts), 1)
        before = jnp.where(ri < ci, 1.0, 0.0).astype(BF16)
        wrh = wrh_ref[...]
        for s in range(ns):
            tok = slice(s * Ts, (s + 1) * Ts)
            xf = x_ref[tok, :]
            h = _rms(xf, nw_ref[...])
            hb = h.astype(BF16)
            h_ref[tok, :] = hb
            hl = (h - hb.astype(F32)).astype(BF16)
            logits = (lax.dot_general(wrh, hb, nt, preferred_element_type=F32)
                      + lax.dot_general(wrh, hl, nt, preferred_element_type=F32)
                      + lax.dot_general(wrl_ref[...], hb, nt, preferred_element_type=F32))
            ei = lax.broadcasted_iota(jnp.int32, logits.shape, 0)
            m1 = jnp.max(logits, axis=0, keepdims=True)
            i1 = jnp.min(jnp.where(logits == m1, ei, N_EXPERTS), axis=0, keepdims=True)
            s1 = ei == i1
            rest = jnp.where(s1, -jnp.inf, logits)
            m2 = jnp.max(rest, axis=0, keepdims=True)
            i2 = jnp.min(jnp.where(rest == m2, ei, N_EXPERTS), axis=0, keepdims=True)
            s2 = ei == i2
            e2 = jnp.exp(m2 - m1)
            w1 = 1.0 / (1.0 + e2)
            gate_ref[s] = jnp.where(s1, w1, 0.0) + jnp.where(s2, e2 * w1, 0.0)
            sel = jnp.where(s1 | s2, 1.0, 0.0)
            rank = jnp.dot(sel.astype(BF16), before, preferred_element_type=F32)
            rank_ref[s] = jnp.where(sel > 0.0, rank, -1.0)
            cnt = jnp.sum(sel, axis=1, keepdims=True)
            for ee in range(N_EXPERTS):
                cnt_ref[s * N_EXPERTS + ee] = cnt[ee, 0].astype(jnp.int32)
            o_ref[tok, :] = xf

    for s in range(ns):
        tok = slice(s * Ts, (s + 1) * Ts)
        n_unit = (cnt_ref[s * N_EXPERTS + e] + MOE_UNIT - 1) // MOE_UNIT
        rem = n_unit % 3
        n_small = jnp.where(n_unit == 1, 1, jnp.where(rem == 0, 0, jnp.where(rem == 2, 1, 2)))
        n_large = (jnp.maximum(n_unit, 2) - 2 * n_small) // 3
        large_base = n_small * MOE_SMALL_BLOCK
        rk = rank_ref[s, pl.ds(e, 1), :]

        def for_blocks(fn, n_small=n_small, n_large=n_large, large_base=large_base):
            def small_body(j, carry):
                fn(pl.multiple_of(j * MOE_SMALL_BLOCK, MOE_SMALL_BLOCK), MOE_SMALL_BLOCK)
                return carry
            lax.fori_loop(0, n_small, small_body, 0)

            def large_body(j, carry):
                fn(pl.multiple_of(large_base + j * MOE_LARGE_BLOCK, MOE_UNIT), MOE_LARGE_BLOCK)
                return carry
            lax.fori_loop(0, n_large, large_body, 0)

        def one_hot(first, size, value, rk=rk):
            rows = (lax.broadcasted_iota(jnp.int32, (size, Ts), 0) + first).astype(F32)
            return jnp.where(rk == rows, value, 0.0).astype(BF16)

        @pl.when(f == 0)
        def _gather(s=s, tok=tok, for_blocks=for_blocks, one_hot=one_hot):
            def gather_block(first, size):
                xe_ref[s, pl.ds(first, size), :] = jnp.dot(
                    one_hot(first, size, 1.0), h_ref[tok, :], preferred_element_type=F32).astype(BF16)
                ye_ref[s, pl.ds(first, size), :] = jnp.zeros((size, D_MODEL), F32)
            for_blocks(gather_block)

        def expert_block(first, size, s=s):
            rows = pl.ds(first, size)
            xb = xe_ref[s, rows, :]
            g = jnp.dot(xb, wg_ref[0], preferred_element_type=F32)
            u = jnp.dot(xb, wu_ref[0], preferred_element_type=F32)
            ye_ref[s, rows, :] += jnp.dot((_silu(g) * u).astype(BF16), wd_ref[0],
                                          preferred_element_type=F32)
        for_blocks(expert_block)

        @pl.when(f == nf - 1)
        def _scatter(s=s, tok=tok, for_blocks=for_blocks, one_hot=one_hot):
            gt = gate_ref[s, pl.ds(e, 1), :]

            def scatter_block(first, size):
                o_ref[tok, :] += lax.dot_general(
                    one_hot(first, size, gt), ye_ref[s, pl.ds(first, size), :].astype(BF16),
                    tn, preferred_element_type=F32)
            for_blocks(scatter_block)

    @pl.when((e == N_EXPERTS - 1) & (f == nf - 1))
    def _final():
        o_ref[...] = _rms(o_ref[...], nfin_ref[...])


def _moe(x, nw, wr_hi, wr_lo, wg, wu, wd, n_final):
    T = x.shape[0]
    tf = MOE_FFN_BLOCK
    nf = wg.shape[2] // tf
    ts = min(MOE_SUB_TILE, T)
    ns = min(MOE_SUB_TILES, T // ts)
    tm = ns * ts
    cap = max(-(-ts // MOE_UNIT) * MOE_UNIT, MOE_SMALL_BLOCK)
    tok = pl.BlockSpec((tm, D_MODEL), lambda i, e, f: (i, 0), pipeline_mode=pl.Buffered(1))
    vec = pl.BlockSpec((1, D_MODEL), lambda i, e, f: (0, 0))
    wr = pl.BlockSpec((N_EXPERTS, D_MODEL), lambda i, e, f: (0, 0))
    w_in = pl.BlockSpec((1, D_MODEL, tf), lambda i, e, f: (e, 0, f))
    return pl.pallas_call(
        functools.partial(_moe_kernel, nf, ns),
        grid=(T // tm, N_EXPERTS, nf),
        in_specs=[
            tok, vec, wr, wr, w_in, w_in,
            pl.BlockSpec((1, tf, D_MODEL), lambda i, e, f: (e, f, 0)),
            vec,
        ],
        out_specs=tok,
        out_shape=jax.ShapeDtypeStruct((T, D_MODEL), F32),
        scratch_shapes=[
            pltpu.VMEM((tm, D_MODEL), BF16),
            pltpu.VMEM((ns, N_EXPERTS, ts), F32),
            pltpu.VMEM((ns, N_EXPERTS, ts), F32),
            pltpu.SMEM((ns * N_EXPERTS,), jnp.int32),
            pltpu.VMEM((ns, cap, D_MODEL), BF16),
            pltpu.VMEM((ns, cap, D_MODEL), F32),
        ],
        compiler_params=_params(("parallel", "arbitrary", "arbitrary"), 58),
        name="moe",
    )(x, nw, wr_hi, wr_lo, wg, wu, wd, n_final)


def _pad_lanes(v, width=128):
    v = v.reshape(1, -1).astype(F32)
    return jnp.pad(v, ((0, 0), (0, width - v.shape[1])))


def _prepare(norm_mix, w_in, sgu_ln_gain, sgu_ln_bias, sgu_w_spatial, sgu_b_spatial,
             dn_conv_w, dn_a_log, dn_dt_bias, dn_out_norm, w_out, norm_ffn,
             ffn_w_gate, ffn_w_up, ffn_w_down, moe_w_router, moe_w_gate, moe_w_up,
             moe_w_down, norm_final):
    depth = w_in.shape[0]
    W = D_MODEL
    layers = []
    for i in range(depth):
        wi = w_in[i]
        seg = lambda a, b: wi[:, a:b]
        ab0 = 6 * W
        ab1 = ab0 + 2 * N_DIR * N_HEADS
        w_main = jnp.concatenate(
            [seg(2 * W, 5 * W), seg(0, 2 * W), seg(5 * W, 6 * W), seg(ab1, ab1 + 2 * W)],
            axis=1).astype(BF16)
        w_ab = jnp.pad(seg(ab0, ab1), ((0, 0), (0, 128 - (ab1 - ab0)))).astype(BF16)
        lp = dict(
            norm_mix=norm_mix[i].reshape(1, W), w_main=w_main, w_ab=w_ab,
            ln_g=sgu_ln_gain[i].reshape(1, W), ln_b=sgu_ln_bias[i].reshape(1, W),
            w_sp=sgu_w_spatial[i].astype(BF16), b_spT=jnp.transpose(sgu_b_spatial[i]),
            conv_w=dn_conv_w[i],
            neg_exp_alog=_pad_lanes(-jnp.exp(dn_a_log[i].astype(F32))),
            dt_bias=_pad_lanes(dn_dt_bias[i]),
            o_norm=dn_out_norm[i].reshape(1, HEAD_DIM), w_out=w_out[i].astype(BF16),
            norm_ffn=norm_ffn[i].reshape(1, W),
        )
        j = i // 2
        if i % 2 == 0:
            lp.update(wg=ffn_w_gate[j].astype(BF16), wu=ffn_w_up[j].astype(BF16),
                      wd=ffn_w_down[j].astype(BF16))
        else:
            wrT = jnp.transpose(moe_w_router[j]).astype(F32)
            wr_hi = wrT.astype(BF16)
            lp.update(wr_hi=wr_hi, wr_lo=(wrT - wr_hi.astype(F32)).astype(BF16),
                      wg=moe_w_gate[j].astype(BF16), wu=moe_w_up[j].astype(BF16),
                      wd=moe_w_down[j].astype(BF16))
        layers.append(lp)
    return layers, norm_final.reshape(1, W)


def _trunk(x3, layers, n_final):
    B, L, W = x3.shape
    x = x3.reshape(B * L, W)
    depth = len(layers)
    assert depth % 2 == 0, "the final RMSNorm is fused into the last (expert) layer"
    for i, lp in enumerate(layers):
        P, qn, kn, vc, knT, gcol, gT = _proj_prep(
            x, lp["norm_mix"], lp["w_main"], lp["w_ab"], lp["conv_w"], lp["neg_exp_alog"],
            lp["dt_bias"], L)
        o_f, o_b = _dn_scan(qn, kn, vc, knT, gcol, gT, L)
        x = _mix(P, o_f, o_b, x, lp["ln_g"], lp["ln_b"], lp["w_sp"], lp["b_spT"],
                 lp["o_norm"], lp["w_out"])
        if i % 2 == 0:
            x = _ffn(x, lp["norm_ffn"], lp["wg"], lp["wu"], lp["wd"])
        else:
            assert i == depth - 1
            x = _moe(x, lp["norm_ffn"], lp["wr_hi"], lp["wr_lo"], lp["wg"], lp["wu"], lp["wd"],
                     n_final)
    return x.reshape(B, L, W)


def kernel(x_prompt, x_sample, norm_mix, w_in, sgu_ln_gain, sgu_ln_bias, sgu_w_spatial, sgu_b_spatial, dn_conv_w, dn_a_log, dn_dt_bias, dn_out_norm, w_out, norm_ffn, ffn_w_gate, ffn_w_up, ffn_w_down, moe_w_router, moe_w_gate, moe_w_up, moe_w_down, norm_final):
    layers, n_final = _prepare(norm_mix, w_in, sgu_ln_gain, sgu_ln_bias, sgu_w_spatial,
                               sgu_b_spatial, dn_conv_w, dn_a_log, dn_dt_bias, dn_out_norm,
                               w_out, norm_ffn, ffn_w_gate, ffn_w_up, ffn_w_down, moe_w_router,
                               moe_w_gate, moe_w_up, moe_w_down, norm_final)
    return (_trunk(x_prompt, layers, n_final), _trunk(x_sample, layers, n_final))
```

```python
import functools

import jax
import jax.numpy as jnp
from jax import lax
from jax.experimental import pallas as pl
from jax.experimental.pallas import tpu as pltpu

F32 = jnp.float32
BF16 = jnp.bfloat16
EPS = 1e-6

D_MODEL = 1024
N_HEADS = 8
HEAD_DIM = 128
N_DIR = 2
CONV_K = 5
CONV_HALO = 8
SGU_CHUNK = 128
SGU_GROUPS = 8
N_EXPERTS = 8
SCAN_CHUNK = 128
MOE_UNIT = 96
MOE_SMALL_BLOCK = 2 * MOE_UNIT
MOE_LARGE_BLOCK = 3 * MOE_UNIT
MOE_SUB_TILE = 1024
MOE_SUB_TILES = 2
MOE_FFN_BLOCK = 896
NEG_BIG = -1e30


def _params(sem, vmem_mb):
    return pltpu.CompilerParams(dimension_semantics=sem, vmem_limit_bytes=vmem_mb * 2**20)


def _dot(a, b):
    return jnp.dot(a.astype(BF16), b.astype(BF16), preferred_element_type=F32)


def _rms(xf, w):
    return xf * lax.rsqrt(jnp.mean(xf * xf, axis=-1, keepdims=True) + EPS) * w


def _silu(x):
    return x * jax.nn.sigmoid(x)


def _gelu(x):
    return 0.5 * x * (1.0 + lax.erf(x * (2.0 ** -0.5)))


PROJ_TILE = 512


def _proj_prep_kernel(tiles_per_seq, x_ref, xp_ref, xn_ref, nw_ref, w_ref, wab_ref, cw_ref, nea_ref,
                      dtb_ref, p_ref, qn_ref, kn_ref, vc_ref, knT_ref, gcol_ref, gT_ref, buf_ref):
    tm = x_ref.shape[0]
    E = CONV_HALO
    half = CONV_K // 2
    pos = pl.program_id(0) % tiles_per_seq
    keep_prev = jnp.where(pos == 0, 0.0, 1.0)
    keep_next = jnp.where(pos == tiles_per_seq - 1, 0.0, 1.0)
    q_scale = HEAD_DIM ** -0.5
    nw = nw_ref[...]
    hb = _rms(x_ref[...], nw).astype(BF16)
    h_halo = _rms(jnp.concatenate([xp_ref[...], xn_ref[...]], axis=0), nw).astype(BF16)

    rest = iter(range(p_ref.shape[1] // D_MODEL))

    def project_rest(count):
        for _ in range(count):
            j = next(rest, None)
            if j is not None:
                cs = slice((3 + j) * D_MODEL, (4 + j) * D_MODEL)
                p_ref[:, j * D_MODEL:(j + 1) * D_MODEL] = jnp.dot(
                    hb, w_ref[:, cs], preferred_element_type=F32).astype(BF16)

    for s in range(3):
        cs = slice(s * D_MODEL, (s + 1) * D_MODEL)
        proj = jnp.dot(hb, w_ref[:, cs], preferred_element_type=F32).astype(BF16).astype(F32)
        halo = jnp.dot(h_halo, w_ref[:, cs], preferred_element_type=F32).astype(BF16).astype(F32)
        project_rest(2)
        buf_ref[0:E, :] = halo[0:E] * keep_prev
        buf_ref[E:E + tm, :] = proj
        buf_ref[E + tm:2 * E + tm, :] = halo[E:2 * E] * keep_next
        acc = None
        for k in range(CONV_K):
            start = E - half + k
            term = buf_ref[start:start + tm, :] * cw_ref[k:k + 1, cs]
            acc = term if acc is None else acc + term
        y = _silu(acc)
        if s == 2:
            vc_ref[...] = y.astype(BF16)
            continue
        for h in range(N_HEADS):
            hs = slice(h * HEAD_DIM, (h + 1) * HEAD_DIM)
            slab = y[:, hs]
            slab = slab * lax.rsqrt(jnp.sum(slab * slab, axis=-1, keepdims=True) + EPS)
            if s == 0:
                qn_ref[:, hs] = (slab * q_scale).astype(BF16)
            else:
                kn_ref[:, hs] = slab.astype(BF16)
                knT_ref[hs, :] = jnp.transpose(slab).astype(BF16)
    project_rest(p_ref.shape[1] // D_MODEL)

    ab = jnp.dot(hb, wab_ref[...], preferred_element_type=F32)
    z = ab + dtb_ref[...]
    softplus = jnp.maximum(z, 0.0) + jnp.log(1.0 + jnp.exp(-jnp.abs(z)))
    g = nea_ref[...] * softplus
    beta = jax.nn.sigmoid(ab)
    ri = lax.broadcasted_iota(jnp.int32, (tm, tm), 0)
    ci = lax.broadcasted_iota(jnp.int32, (tm, tm), 1)
    same = jnp.right_shift(ri, 7) == jnp.right_shift(ci, 7)
    lower = jnp.where(same & (ci <= ri), 1.0, 0.0).astype(BF16)
    upper = jnp.where(same & (ci >= ri), 1.0, 0.0).astype(BF16)
    g_hi = g.astype(BF16)
    g_lo = (g - g_hi.astype(F32)).astype(BF16)
    g_fwd = (jnp.dot(lower, g_hi, preferred_element_type=F32)
             + jnp.dot(lower, g_lo, preferred_element_type=F32))
    g_bwd = (jnp.dot(upper, g_hi, preferred_element_type=F32)
             + jnp.dot(upper, g_lo, preferred_element_type=F32))
    lane = lax.broadcasted_iota(jnp.int32, g.shape, 1)
    gcol = jnp.where(lane < N_HEADS, g_fwd, jnp.where(lane < N_DIR * N_HEADS, g_bwd, beta))
    gcol_ref[...] = gcol
    gT_ref[...] = jnp.transpose(gcol)[0:N_DIR * N_HEADS, :]


def _proj_prep(x, nw, w_main, w_ab, conv_w, neg_exp_alog, dt_bias, seq_len):
    T = x.shape[0]
    tm = min(PROJ_TILE, seq_len)
    tiles_per_seq = seq_len // tm
    hb = tm // CONV_HALO
    n_hb = T // CONV_HALO
    n_cols = w_main.shape[1]
    rest = n_cols - 3 * D_MODEL
    resident = lambda shape: pl.BlockSpec(shape, lambda i: (0, 0), pipeline_mode=pl.Buffered(1))
    vec = lambda n: pl.BlockSpec((1, n), lambda i: (0, 0))
    tok = jax.ShapeDtypeStruct((T, D_MODEL), BF16)
    tok_spec = pl.BlockSpec((tm, D_MODEL), lambda i: (i, 0))
    return pl.pallas_call(
        functools.partial(_proj_prep_kernel, tiles_per_seq),
        grid=(T // tm,),
        in_specs=[
            tok_spec,
            pl.BlockSpec((CONV_HALO, D_MODEL), lambda i: (jnp.maximum(i * hb - 1, 0), 0)),
            pl.BlockSpec((CONV_HALO, D_MODEL), lambda i: (jnp.minimum((i + 1) * hb, n_hb - 1), 0)),
            vec(D_MODEL),
            resident((D_MODEL, n_cols)),
            resident((D_MODEL, 128)),
            pl.BlockSpec((CONV_K, 3 * D_MODEL), lambda i: (0, 0)),
            vec(128), vec(128),
        ],
        out_specs=[
            pl.BlockSpec((tm, rest), lambda i: (i, 0)),
            tok_spec, tok_spec, tok_spec,
            pl.BlockSpec((D_MODEL, tm), lambda i: (0, i)),
            pl.BlockSpec((tm, 128), lambda i: (i, 0)),
            pl.BlockSpec((N_DIR * N_HEADS, tm), lambda i: (0, i)),
        ],
        out_shape=[jax.ShapeDtypeStruct((T, rest), BF16), tok, tok, tok,
                   jax.ShapeDtypeStruct((D_MODEL, T), BF16),
                   jax.ShapeDtypeStruct((T, 128), F32),
                   jax.ShapeDtypeStruct((N_DIR * N_HEADS, T), F32)],
        scratch_shapes=[pltpu.VMEM((tm + 2 * CONV_HALO, D_MODEL), F32)],
        compiler_params=_params(("parallel",), 56),
        name="proj_prep",
    )(x, x, x, nw, w_main, w_ab, conv_w, neg_exp_alog, dt_bias)


def _stack(a, b):
    return jnp.concatenate([a.astype(BF16), b.astype(BF16)], axis=0)


def _merge_rows(direction, blk):
    first = blk if direction == 0 else 0
    return [slice(s, s + blk) for s in range(first, SCAN_CHUNK, 2 * blk)]


def _dn_scan_kernel(qf_ref, kf_ref, vf_ref, kTf_ref, gcf_ref, gTf_ref,
                    qb_ref, kb_ref, vb_ref, kTb_ref, gcb_ref, gTb_ref,
                    of_ref, ob_ref, S_ref):
    CH = SCAN_CHUNK
    NJ = qf_ref.shape[0] // CH
    refs = ((qf_ref, kf_ref, vf_ref, kTf_ref, gcf_ref, gTf_ref, of_ref),
            (qb_ref, kb_ref, vb_ref, kTb_ref, gcb_ref, gTb_ref, ob_ref))
    chains = [(d, j, h) for d in range(N_DIR) for j in range(NJ) for h in range(N_HEADS)]
    C = range(len(chains))
    tok = lambda j: slice(j * CH, (j + 1) * CH)

    @pl.when(pl.program_id(1) == 0)
    def _():
        S_ref[...] = jnp.zeros_like(S_ref)

    ri = lax.broadcasted_iota(jnp.int32, (CH, CH), 0)
    ci = lax.broadcasted_iota(jnp.int32, (CH, CH), 1)
    incl = (ri >= ci, ri <= ci)
    strict = (ri > ci, ri < ci)
    end = (CH - 1, 0)
    eye = jnp.where(ri == ci, 1.0, 0.0)
    same = lambda sh: jnp.right_shift(ri, sh) == jnp.right_shift(ci, sh)
    m16, m32, m64 = same(4), same(5), same(6)
    levels = ((16, m32 & ~m16), (32, m64 & ~m32), (64, ~m64))
    hs = lambda h: slice(h * HEAD_DIM, (h + 1) * HEAD_DIM)
    gc = [r[4][...] for r in refs]
    gT = [r[5][...] for r in refs]

    def col(d, j, h, base):
        l = base + d * N_HEADS + h
        return gc[d][tok(j), l:l + 1]

    Gc = [col(d, j, h, 0) for d, j, h in chains]
    beta = [col(d, j, h, N_DIR * N_HEADS) for d, j, h in chains]
    Gr = [gT[d][d * N_HEADS + h:d * N_HEADS + h + 1, tok(j)] for d, j, h in chains]
    G_end = [Gr[c][:, end[chains[c][0]]:end[chains[c][0]] + 1] for c in C]
    q = [refs[d][0][tok(j), hs(h)] for d, j, h in chains]
    k = [refs[d][1][tok(j), hs(h)] for d, j, h in chains]
    kT = [refs[d][3][hs(h), tok(j)] for d, j, h in chains]

    decay = [jnp.exp(jnp.where(incl[chains[c][0]], Gc[c] - Gr[c], NEG_BIG)) for c in C]
    KQ = [jnp.dot(jnp.concatenate([k[c], q[c]], axis=0), kT[c], preferred_element_type=F32) for c in C]
    A = [jnp.where(strict[chains[c][0]], beta[c] * KQ[c][:CH] * decay[c], 0.0) for c in C]
    qkm = [(KQ[c][CH:] * decay[c]).astype(BF16) for c in C]
    Dg = [jnp.where(m16, A[c], 0.0) for c in C]
    D2 = [_dot(Dg[c], Dg[c]) for c in C]
    X = [eye - Dg[c] for c in C]
    Y = [_dot(_stack(D2[c], X[c]), D2[c]) for c in C]
    D4 = [Y[c][:CH] for c in C]
    X = [X[c] + Y[c][CH:] for c in C]
    Y = [_dot(_stack(D4[c], X[c]), D4[c]) for c in C]
    X = [X[c] + Y[c][CH:] for c in C]
    X = [X[c] + _dot(X[c], Y[c][:CH]) for c in C]
    for blk, off in levels:
        rows = [_merge_rows(chains[c][0], blk) for c in C]
        Xr = [jnp.concatenate([X[c][r] for r in rows[c]], axis=0) for c in C]
        Z = [_dot(Xr[c], jnp.where(off, A[c], 0.0)) for c in C]
        Xr = [Xr[c] - _dot(Z[c], X[c]) for c in C]
        Xn = []
        for c in C:
            parts, pos = [], 0
            for j, r in enumerate(rows[c]):
                if r.start > pos:
                    parts.append(X[c][pos:r.start])
                parts.append(Xr[c][j * blk:(j + 1) * blk])
                pos = r.stop
            if pos < CH:
                parts.append(X[c][pos:CH])
            Xn.append(jnp.concatenate(parts, axis=0))
        X = Xn

    eG = [jnp.exp(Gc[c]) for c in C]
    kbe = [(k[c].astype(F32) * (beta[c] * eG[c])).astype(BF16) for c in C]
    vbeta = [refs[d][2][tok(j), hs(h)].astype(F32) * beta[c] for c, (d, j, h) in enumerate(chains)]
    qg = [(q[c].astype(F32) * eG[c]).astype(BF16) for c in C]
    keT = [(kT[c].astype(F32) * jnp.exp(G_end[c] - Gr[c])).astype(BF16) for c in C]
    dec = [jnp.exp(G_end[c]) for c in C]

    heads = [(d, h) for d in range(N_DIR) for h in range(N_HEADS)]
    Hd = range(len(heads))
    S = [S_ref[i] for i in Hd]
    for t in range(NJ):
        cur = [chains.index((d, t if d == 0 else NJ - 1 - t, h)) for d, h in heads]
        Sb = [S[i].astype(BF16) for i in Hd]
        kS = [jnp.dot(kbe[cur[i]], Sb[i], preferred_element_type=F32) for i in Hd]
        vnb = [_dot(X[cur[i]], vbeta[cur[i]] - kS[i]).astype(BF16) for i in Hd]
        o = [jnp.dot(jnp.concatenate([qg[cur[i]], qkm[cur[i]]], axis=1),
                     jnp.concatenate([Sb[i], vnb[i]], axis=0), preferred_element_type=F32) for i in Hd]
        dS = [jnp.dot(keT[cur[i]], vnb[i], preferred_element_type=F32) for i in Hd]
        for i in Hd:
            d, j, h = chains[cur[i]]
            refs[d][6][tok(j), hs(h)] = o[i].astype(refs[d][6].dtype)
        S = [S[i] * dec[cur[i]] + dS[i] for i in Hd]
    for i in Hd:
        S_ref[i] = S[i]


SCAN_BLOCK = 2 * SCAN_CHUNK


def _dn_scan(qn, kn, vc, knT, gcol, gT, seq_len):
    T = qn.shape[0]
    CH = min(SCAN_BLOCK, seq_len)
    nc = seq_len // CH
    nb = T // seq_len
    fwd = lambda b, c: b * nc + c
    bwd = lambda b, c: b * nc + (nc - 1 - c)

    def specs(blk):
        tok = pl.BlockSpec((CH, D_MODEL), lambda b, c: (blk(b, c), 0))
        return [tok, tok, tok,
                pl.BlockSpec((D_MODEL, CH), lambda b, c: (0, blk(b, c))),
                pl.BlockSpec((CH, 128), lambda b, c: (blk(b, c), 0)),
                pl.BlockSpec((N_DIR * N_HEADS, CH), lambda b, c: (0, blk(b, c)))]

    out = jax.ShapeDtypeStruct((T, D_MODEL), BF16)
    args = (qn, kn, vc, knT, gcol, gT)
    return pl.pallas_call(
        _dn_scan_kernel,
        grid=(nb, nc),
        in_specs=specs(fwd) + specs(bwd),
        out_specs=[specs(fwd)[0], specs(bwd)[0]],
        out_shape=[out, out],
        scratch_shapes=[pltpu.VMEM((N_DIR * N_HEADS, HEAD_DIM, HEAD_DIM), F32)],
        compiler_params=_params(("parallel", "arbitrary"), 32),
        name="dn_scan",
    )(*args, *args)


def _mix_kernel(u_ref, vs_ref, z_ref, ga_ref, gb_ref, of_ref, ob_ref, x_ref,
                lng_ref, lnb_ref, ws_ref, bs_ref, on_ref, wo_ref, out_ref, mix_ref):
    tm = x_ref.shape[0]
    v = _gelu(vs_ref[...].astype(F32))
    vcen = v - jnp.mean(v, axis=-1, keepdims=True)
    vln = vcen * lax.rsqrt(jnp.mean(vcen * vcen, axis=-1, keepdims=True) + EPS)
    vb = (vln * lng_ref[...] + lnb_ref[...]).astype(BF16)
    for g in range(SGU_GROUPS):
        gs = slice(g * HEAD_DIM, (g + 1) * HEAD_DIM)
        u = _gelu(u_ref[:, gs].astype(F32))
        gate_a = jax.nn.sigmoid(ga_ref[:, gs].astype(F32))
        o = of_ref[:, gs].astype(F32) + ob_ref[:, gs].astype(F32)
        o = o * lax.rsqrt(jnp.mean(o * o, axis=-1, keepdims=True) + EPS) * on_ref[...]
        o = o * _silu(z_ref[:, gs].astype(F32))
        mixed = jax.nn.sigmoid(gb_ref[:, gs].astype(F32)) * o
        w_sp = ws_ref[g]
        bias = bs_ref[:, g:g + 1]
        for c in range(tm // SGU_CHUNK):
            rs = slice(c * SGU_CHUNK, (c + 1) * SGU_CHUNK)
            s = jnp.dot(w_sp, vb[rs, gs], preferred_element_type=F32) + bias
            mix_ref[rs, gs] = (mixed[rs] + gate_a[rs] * (u[rs] * s)).astype(BF16)
    out_ref[...] = x_ref[...] + jnp.dot(mix_ref[...], wo_ref[...], preferred_element_type=F32)


def _mix(P, o_f, o_b, x, ln_g, ln_b, w_sp, b_spT, o_norm, w_out):
    T = x.shape[0]
    tm = min(256, T)
    col = lambda j: pl.BlockSpec((tm, D_MODEL), lambda i: (i, j))
    tok = pl.BlockSpec((tm, D_MODEL), lambda i: (i, 0))
    vec = pl.BlockSpec((1, D_MODEL), lambda i: (0, 0))
    return pl.pallas_call(
        _mix_kernel,
        grid=(T // tm,),
        in_specs=[
            col(0), col(1), col(2), col(3), col(4), tok, tok, tok, vec, vec,
            pl.BlockSpec((SGU_GROUPS, SGU_CHUNK, SGU_CHUNK), lambda i: (0, 0, 0)),
            pl.BlockSpec((SGU_CHUNK, SGU_GROUPS), lambda i: (0, 0)),
            pl.BlockSpec((1, HEAD_DIM), lambda i: (0, 0)),
            pl.BlockSpec((D_MODEL, D_MODEL), lambda i: (0, 0)),
        ],
        out_specs=tok,
        out_shape=jax.ShapeDtypeStruct((T, D_MODEL), F32),
        scratch_shapes=[pltpu.VMEM((tm, D_MODEL), BF16)],
        compiler_params=_params(("parallel",), 48),
        name="mix",
    )(P, P, P, P, P, o_f, o_b, x, ln_g, ln_b, w_sp, b_spT, o_norm, w_out)


FFN_BLOCK = 256


def _ffn_kernel(x_ref, nw_ref, wg_ref, wu_ref, wd_ref, o_ref):
    xf = x_ref[...]
    h = _rms(xf, nw_ref[...]).astype(BF16)
    o_ref[...] = xf
    for f in range(wg_ref.shape[1] // FFN_BLOCK):
        fs = slice(f * FFN_BLOCK, (f + 1) * FFN_BLOCK)
        g = jnp.dot(h, wg_ref[:, fs], preferred_element_type=F32)
        u = jnp.dot(h, wu_ref[:, fs], preferred_element_type=F32)
        o_ref[...] += jnp.dot((_silu(g) * u).astype(BF16), wd_ref[fs, :], preferred_element_type=F32)


def _ffn(x, nw, wg, wu, wd):
    T = x.shape[0]
    F = wg.shape[1]
    tm = min(512, T)
    tok = pl.BlockSpec((tm, D_MODEL), lambda i: (i, 0))
    resident = lambda shape: pl.BlockSpec(shape, lambda i: (0, 0), pipeline_mode=pl.Buffered(1))
    return pl.pallas_call(
        _ffn_kernel,
        grid=(T // tm,),
        in_specs=[
            tok,
            pl.BlockSpec((1, D_MODEL), lambda i: (0, 0)),
            resident((D_MODEL, F)), resident((D_MODEL, F)), resident((F, D_MODEL)),
        ],
        out_specs=tok,
        out_shape=jax.ShapeDtypeStruct((T, D_MODEL), F32),
        compiler_params=_params(("parallel",), 52),
        name="ffn",
    )(x, nw, wg, wu, wd)


def _moe_kernel(nf, ns, x_ref, nw_ref, wrh_ref, wrl_ref, wgu_ref, wd_ref, nfin_ref, o_ref,
                h_ref, rank_ref, gate_ref, cnt_ref, xe_ref, ye_ref):
    e = pl.program_id(1)
    f = pl.program_id(2)
    Ts = x_ref.shape[0] // ns
    nt = (((1,), (1,)), ((), ()))
    tn = (((0,), (0,)), ((), ()))

    @pl.when((e == 0) & (f == 0))
    def _route():
        ri = lax.broadcasted_iota(jnp.int32, (Ts, Ts), 0)
        ci = lax.broadcasted_iota(jnp.int32, (Ts, Ts), 1)
        before = jnp.where(ri < ci, 1.0, 0.0).astype(BF16)
        wrh = wrh_ref[...]
        for s in range(ns):
            tok = slice(s * Ts, (s + 1) * Ts)
            xf = x_ref[tok, :]
            h = _rms(xf, nw_ref[...])
            hb = h.astype(BF16)
            h_ref[tok, :] = hb
            hl = (h - hb.astype(F32)).astype(BF16)
            logits = (lax.dot_general(wrh, hb, nt, preferred_element_type=F32)
                      + lax.dot_general(wrh, hl, nt, preferred_element_type=F32)
                      + lax.dot_general(wrl_ref[...], hb, nt, preferred_element_type=F32))
            ei = lax.broadcasted_iota(jnp.int32, logits.shape, 0)
            m1 = jnp.max(logits, axis=0, keepdims=True)
            i1 = jnp.min(jnp.where(logits == m1, ei, N_EXPERTS), axis=0, keepdims=True)
            s1 = ei == i1
            rest = jnp.where(s1, -jnp.inf, logits)
            m2 = jnp.max(rest, axis=0, keepdims=True)
            i2 = jnp.min(jnp.where(rest == m2, ei, N_EXPERTS), axis=0, keepdims=True)
            s2 = ei == i2
            e2 = jnp.exp(m2 - m1)
            w1 = 1.0 / (1.0 + e2)
            gate_ref[s] = jnp.where(s1, w1, 0.0) + jnp.where(s2, e2 * w1, 0.0)
            sel = jnp.where(s1 | s2, 1.0, 0.0)
            rank = jnp.dot(sel.astype(BF16), before, preferred_element_type=F32)
            rank_ref[s] = jnp.where(sel > 0.0, rank, -1.0)
            cnt = jnp.sum(sel, axis=1, keepdims=True)
            for ee in range(N_EXPERTS):
                cnt_ref[s * N_EXPERTS + ee] = cnt[ee, 0].astype(jnp.int32)
            o_ref[tok, :] = xf

    for s in range(ns):
        tok = slice(s * Ts, (s + 1) * Ts)
        n_unit = (cnt_ref[s * N_EXPERTS + e] + MOE_UNIT - 1) // MOE_UNIT
        rem = n_unit % 3
        n_small = jnp.where(n_unit == 1, 1, jnp.where(rem == 0, 0, jnp.where(rem == 2, 1, 2)))
        n_large = (jnp.maximum(n_unit, 2) - 2 * n_small) // 3
        large_base = n_small * MOE_SMALL_BLOCK
        rk = rank_ref[s, pl.ds(e, 1), :]

        def for_blocks(fn, n_small=n_small, n_large=n_large, large_base=large_base):
            def small_body(j, carry):
                fn(pl.multiple_of(j * MOE_SMALL_BLOCK, MOE_SMALL_BLOCK), MOE_SMALL_BLOCK)
                return carry
            lax.fori_loop(0, n_small, small_body, 0)

            def large_body(j, carry):
                fn(pl.multiple_of(large_base + j * MOE_LARGE_BLOCK, MOE_UNIT), MOE_LARGE_BLOCK)
                return carry
            lax.fori_loop(0, n_large, large_body, 0)

        def one_hot(first, size, value, rk=rk):
            rows = (lax.broadcasted_iota(jnp.int32, (size, Ts), 0) + first).astype(F32)
            return jnp.where(rk == rows, value, 0.0).astype(BF16)

        @pl.when(f == 0)
        def _gather(s=s, tok=tok, for_blocks=for_blocks, one_hot=one_hot):
            def gather_block(first, size):
                xe_ref[s, pl.ds(first, size), :] = jnp.dot(
                    one_hot(first, size, 1.0), h_ref[tok, :], preferred_element_type=F32).astype(BF16)
                ye_ref[s, pl.ds(first, size), :] = jnp.zeros((size, D_MODEL), F32)
            for_blocks(gather_block)

        def expert_block(first, size, s=s):
            rows = pl.ds(first, size)
            xb = xe_ref[s, rows, :]
            gu = jnp.dot(xb, wgu_ref[0], preferred_element_type=F32)
            g, u = gu[:, :MOE_FFN_BLOCK], gu[:, MOE_FFN_BLOCK:]
            ye_ref[s, rows, :] += jnp.dot((_silu(g) * u).astype(BF16), wd_ref[0],
                                          preferred_element_type=F32)
        for_blocks(expert_block)

        @pl.when(f == nf - 1)
        def _scatter(s=s, tok=tok, for_blocks=for_blocks, one_hot=one_hot):
            gt = gate_ref[s, pl.ds(e, 1), :]

            def scatter_block(first, size):
                o_ref[tok, :] += lax.dot_general(
                    one_hot(first, size, gt), ye_ref[s, pl.ds(first, size), :].astype(BF16),
                    tn, preferred_element_type=F32)
            for_blocks(scatter_block)

    @pl.when((e == N_EXPERTS - 1) & (f == nf - 1))
    def _final():
        o_ref[...] = _rms(o_ref[...], nfin_ref[...])


def _moe(x, nw, wr_hi, wr_lo, wgu, wd, n_final):
    T = x.shape[0]
    tf = MOE_FFN_BLOCK
    nf = wd.shape[1] // tf
    ts = min(MOE_SUB_TILE, T)
    ns = min(MOE_SUB_TILES, T // ts)
    tm = ns * ts
    cap = max(-(-ts // MOE_UNIT) * MOE_UNIT, MOE_SMALL_BLOCK)
    tok = pl.BlockSpec((tm, D_MODEL), lambda i, e, f: (i, 0), pipeline_mode=pl.Buffered(1))
    vec = pl.BlockSpec((1, D_MODEL), lambda i, e, f: (0, 0))
    wr = pl.BlockSpec((N_EXPERTS, D_MODEL), lambda i, e, f: (0, 0))
    w_in = pl.BlockSpec((1, D_MODEL, 2 * tf), lambda i, e, f: (e, 0, f))
    return pl.pallas_call(
        functools.partial(_moe_kernel, nf, ns),
        grid=(T // tm, N_EXPERTS, nf),
        in_specs=[
            tok, vec, wr, wr, w_in,
            pl.BlockSpec((1, tf, D_MODEL), lambda i, e, f: (e, f, 0)),
            vec,
        ],
        out_specs=tok,
        out_shape=jax.ShapeDtypeStruct((T, D_MODEL), F32),
        scratch_shapes=[
            pltpu.VMEM((tm, D_MODEL), BF16),
            pltpu.VMEM((ns, N_EXPERTS, ts), F32),
            pltpu.VMEM((ns, N_EXPERTS, ts), F32),
            pltpu.SMEM((ns * N_EXPERTS,), jnp.int32),
            pltpu.VMEM((ns, cap, D_MODEL), BF16),
            pltpu.VMEM((ns, cap, D_MODEL), F32),
        ],
        compiler_params=_params(("parallel", "arbitrary", "arbitrary"), 58),
        name="moe",
    )(x, nw, wr_hi, wr_lo, wgu, wd, n_final)


def _pad_lanes(v, width=128):
    v = v.reshape(1, -1).astype(F32)
    return jnp.pad(v, ((0, 0), (0, width - v.shape[1])))


def _prepare(norm_mix, w_in, sgu_ln_gain, sgu_ln_bias, sgu_w_spatial, sgu_b_spatial,
             dn_conv_w, dn_a_log, dn_dt_bias, dn_out_norm, w_out, norm_ffn,
             ffn_w_gate, ffn_w_up, ffn_w_down, moe_w_router, moe_w_gate, moe_w_up,
             moe_w_down, norm_final):
    depth = w_in.shape[0]
    W = D_MODEL
    layers = []
    for i in range(depth):
        wi = w_in[i]
        seg = lambda a, b: wi[:, a:b]
        ab0 = 6 * W
        ab1 = ab0 + 2 * N_DIR * N_HEADS
        w_main = jnp.concatenate(
            [seg(2 * W, 5 * W), seg(0, 2 * W), seg(5 * W, 6 * W), seg(ab1, ab1 + 2 * W)],
            axis=1).astype(BF16)
        w_ab = jnp.pad(seg(ab0, ab1), ((0, 0), (0, 128 - (ab1 - ab0)))).astype(BF16)
        lp = dict(
            norm_mix=norm_mix[i].reshape(1, W), w_main=w_main, w_ab=w_ab,
            ln_g=sgu_ln_gain[i].reshape(1, W), ln_b=sgu_ln_bias[i].reshape(1, W),
            w_sp=sgu_w_spatial[i].astype(BF16), b_spT=jnp.transpose(sgu_b_spatial[i]),
            conv_w=dn_conv_w[i],
            neg_exp_alog=_pad_lanes(-jnp.exp(dn_a_log[i].astype(F32))),
            dt_bias=_pad_lanes(dn_dt_bias[i]),
            o_norm=dn_out_norm[i].reshape(1, HEAD_DIM), w_out=w_out[i].astype(BF16),
            norm_ffn=norm_ffn[i].reshape(1, W),
        )
        j = i // 2
        if i % 2 == 0:
            lp.update(wg=ffn_w_gate[j].astype(BF16), wu=ffn_w_up[j].astype(BF16),
                      wd=ffn_w_down[j].astype(BF16))
        else:
            wrT = jnp.transpose(moe_w_router[j]).astype(F32)
            wr_hi = wrT.astype(BF16)
            blocks = lambda w: w.astype(BF16).reshape(N_EXPERTS, W, -1, 1, MOE_FFN_BLOCK)
            wgu = jnp.concatenate([blocks(moe_w_gate[j]), blocks(moe_w_up[j])], axis=3)
            lp.update(wr_hi=wr_hi, wr_lo=(wrT - wr_hi.astype(F32)).astype(BF16),
                      wgu=wgu.reshape(N_EXPERTS, W, -1), wd=moe_w_down[j].astype(BF16))
        layers.append(lp)
    return layers, norm_final.reshape(1, W)


def _trunk(x3, layers, n_final):
    B, L, W = x3.shape
    x = x3.reshape(B * L, W)
    depth = len(layers)
    assert depth % 2 == 0, "the final RMSNorm is fused into the last (expert) layer"
    for i, lp in enumerate(layers):
        P, qn, kn, vc, knT, gcol, gT = _proj_prep(
            x, lp["norm_mix"], lp["w_main"], lp["w_ab"], lp["conv_w"], lp["neg_exp_alog"],
            lp["dt_bias"], L)
        o_f, o_b = _dn_scan(qn, kn, vc, knT, gcol, gT, L)
        x = _mix(P, o_f, o_b, x, lp["ln_g"], lp["ln_b"], lp["w_sp"], lp["b_spT"],
                 lp["o_norm"], lp["w_out"])
        if i % 2 == 0:
            x = _ffn(x, lp["norm_ffn"], lp["wg"], lp["wu"], lp["wd"])
        else:
            assert i == depth - 1
            x = _moe(x, lp["norm_ffn"], lp["wr_hi"], lp["wr_lo"], lp["wgu"], lp["wd"], n_final)
    return x.reshape(B, L, W)


def kernel(x_prompt, x_sample, norm_mix, w_in, sgu_ln_gain, sgu_ln_bias, sgu_w_spatial, sgu_b_spatial, dn_conv_w, dn_a_log, dn_dt_bias, dn_out_norm, w_out, norm_ffn, ffn_w_gate, ffn_w_up, ffn_w_down, moe_w_router, moe_w_gate, moe_w_up, moe_w_down, norm_final):
    layers, n_final = _prepare(norm_mix, w_in, sgu_ln_gain, sgu_ln_bias, sgu_w_spatial,
                               sgu_b_spatial, dn_conv_w, dn_a_log, dn_dt_bias, dn_out_norm,
                               w_out, norm_ffn, ffn_w_gate, ffn_w_up, ffn_w_down, moe_w_router,
                               moe_w_gate, moe_w_up, moe_w_down, norm_final)
    return (_trunk(x_prompt, layers, n_final), _trunk(x_sample, layers, n_final))
```

```python
import functools

import jax
import jax.numpy as jnp
from jax import lax
from jax.experimental import pallas as pl
from jax.experimental.pallas import tpu as pltpu

F32 = jnp.float32
BF16 = jnp.bfloat16
EPS = 1e-6

D_MODEL = 1024
N_HEADS = 8
HEAD_DIM = 128
N_DIR = 2
CONV_K = 5
CONV_HALO = 8
SGU_CHUNK = 128
SGU_GROUPS = 8
N_EXPERTS = 8
SCAN_CHUNK = 128
MOE_UNIT = 96
MOE_SMALL_BLOCK = 2 * MOE_UNIT
MOE_LARGE_BLOCK = 3 * MOE_UNIT
MOE_SUB_TILE = 1024
MOE_SUB_TILES = 2
MOE_FFN_BLOCK = 896
NEG_BIG = -1e30


def _params(sem, vmem_mb):
    return pltpu.CompilerParams(dimension_semantics=sem, vmem_limit_bytes=vmem_mb * 2**20)


def _dot(a, b):
    return jnp.dot(a.astype(BF16), b.astype(BF16), preferred_element_type=F32)


def _rms(xf, w):
    return xf * lax.rsqrt(jnp.mean(xf * xf, axis=-1, keepdims=True) + EPS) * w


def _silu(x):
    return x * jax.nn.sigmoid(x)


def _gelu(x):
    return 0.5 * x * (1.0 + lax.erf(x * (2.0 ** -0.5)))


PROJ_TILE = 512


def _proj_prep_kernel(tiles_per_seq, x_ref, xp_ref, xn_ref, nw_ref, w_ref, wab_ref, cw_ref, nea_ref,
                      dtb_ref, p_ref, qn_ref, kn_ref, vc_ref, knT_ref, gcol_ref, gT_ref, buf_ref):
    tm = x_ref.shape[0]
    E = CONV_HALO
    half = CONV_K // 2
    pos = pl.program_id(0) % tiles_per_seq
    keep_prev = jnp.where(pos == 0, 0.0, 1.0)
    keep_next = jnp.where(pos == tiles_per_seq - 1, 0.0, 1.0)
    q_scale = HEAD_DIM ** -0.5
    nw = nw_ref[...]
    hb = _rms(x_ref[...], nw).astype(BF16)
    h_halo = _rms(jnp.concatenate([xp_ref[...], xn_ref[...]], axis=0), nw).astype(BF16)

    rest = iter(range(p_ref.shape[1] // D_MODEL))

    def project_rest(count):
        for _ in range(count):
            j = next(rest, None)
            if j is not None:
                cs = slice((3 + j) * D_MODEL, (4 + j) * D_MODEL)
                p_ref[:, j * D_MODEL:(j + 1) * D_MODEL] = jnp.dot(
                    hb, w_ref[:, cs], preferred_element_type=F32).astype(BF16)

    for s in range(3):
        cs = slice(s * D_MODEL, (s + 1) * D_MODEL)
        proj = jnp.dot(hb, w_ref[:, cs], preferred_element_type=F32).astype(BF16).astype(F32)
        halo = jnp.dot(h_halo, w_ref[:, cs], preferred_element_type=F32).astype(BF16).astype(F32)
        project_rest(2)
        buf_ref[0:E, :] = halo[0:E] * keep_prev
        buf_ref[E:E + tm, :] = proj
        buf_ref[E + tm:2 * E + tm, :] = halo[E:2 * E] * keep_next
        acc = None
        for k in range(CONV_K):
            start = E - half + k
            term = buf_ref[start:start + tm, :] * cw_ref[k:k + 1, cs]
            acc = term if acc is None else acc + term
        y = _silu(acc)
        if s == 2:
            vc_ref[...] = y.astype(BF16)
            continue
        for h in range(N_HEADS):
            hs = slice(h * HEAD_DIM, (h + 1) * HEAD_DIM)
            slab = y[:, hs]
            slab = slab * lax.rsqrt(jnp.sum(slab * slab, axis=-1, keepdims=True) + EPS)
            if s == 0:
                qn_ref[:, hs] = (slab * q_scale).astype(BF16)
            else:
                kn_ref[:, hs] = slab.astype(BF16)
                knT_ref[hs, :] = jnp.transpose(slab).astype(BF16)
    project_rest(p_ref.shape[1] // D_MODEL)

    ab = jnp.dot(hb, wab_ref[...], preferred_element_type=F32)
    z = ab + dtb_ref[...]
    softplus = jnp.maximum(z, 0.0) + jnp.log(1.0 + jnp.exp(-jnp.abs(z)))
    g = nea_ref[...] * softplus
    beta = jax.nn.sigmoid(ab)
    ri = lax.broadcasted_iota(jnp.int32, (tm, tm), 0)
    ci = lax.broadcasted_iota(jnp.int32, (tm, tm), 1)
    same = jnp.right_shift(ri, 7) == jnp.right_shift(ci, 7)
    lower = jnp.where(same & (ci <= ri), 1.0, 0.0).astype(BF16)
    upper = jnp.where(same & (ci >= ri), 1.0, 0.0).astype(BF16)
    g_hi = g.astype(BF16)
    g_lo = (g - g_hi.astype(F32)).astype(BF16)
    g_fwd = (jnp.dot(lower, g_hi, preferred_element_type=F32)
             + jnp.dot(lower, g_lo, preferred_element_type=F32))
    g_bwd = (jnp.dot(upper, g_hi, preferred_element_type=F32)
             + jnp.dot(upper, g_lo, preferred_element_type=F32))
    lane = lax.broadcasted_iota(jnp.int32, g.shape, 1)
    gcol = jnp.where(lane < N_HEADS, g_fwd, jnp.where(lane < N_DIR * N_HEADS, g_bwd, beta))
    gcol_ref[...] = gcol
    gT_ref[...] = jnp.transpose(gcol)[0:N_DIR * N_HEADS, :]


def _proj_prep(x, nw, w_main, w_ab, conv_w, neg_exp_alog, dt_bias, seq_len):
    T = x.shape[0]
    tm = min(PROJ_TILE, seq_len)
    tiles_per_seq = seq_len // tm
    hb = tm // CONV_HALO
    n_hb = T // CONV_HALO
    n_cols = w_main.shape[1]
    rest = n_cols - 3 * D_MODEL
    resident = lambda shape: pl.BlockSpec(shape, lambda i: (0, 0), pipeline_mode=pl.Buffered(1))
    vec = lambda n: pl.BlockSpec((1, n), lambda i: (0, 0))
    tok = jax.ShapeDtypeStruct((T, D_MODEL), BF16)
    tok_spec = pl.BlockSpec((tm, D_MODEL), lambda i: (i, 0))
    return pl.pallas_call(
        functools.partial(_proj_prep_kernel, tiles_per_seq),
        grid=(T // tm,),
        in_specs=[
            tok_spec,
            pl.BlockSpec((CONV_HALO, D_MODEL), lambda i: (jnp.maximum(i * hb - 1, 0), 0)),
            pl.BlockSpec((CONV_HALO, D_MODEL), lambda i: (jnp.minimum((i + 1) * hb, n_hb - 1), 0)),
            vec(D_MODEL),
            resident((D_MODEL, n_cols)),
            resident((D_MODEL, 128)),
            pl.BlockSpec((CONV_K, 3 * D_MODEL), lambda i: (0, 0)),
            vec(128), vec(128),
        ],
        out_specs=[
            pl.BlockSpec((tm, rest), lambda i: (i, 0)),
            tok_spec, tok_spec, tok_spec,
            pl.BlockSpec((D_MODEL, tm), lambda i: (0, i)),
            pl.BlockSpec((tm, 128), lambda i: (i, 0)),
            pl.BlockSpec((N_DIR * N_HEADS, tm), lambda i: (0, i)),
        ],
        out_shape=[jax.ShapeDtypeStruct((T, rest), BF16), tok, tok, tok,
                   jax.ShapeDtypeStruct((D_MODEL, T), BF16),
                   jax.ShapeDtypeStruct((T, 128), F32),
                   jax.ShapeDtypeStruct((N_DIR * N_HEADS, T), F32)],
        scratch_shapes=[pltpu.VMEM((tm + 2 * CONV_HALO, D_MODEL), F32)],
        compiler_params=_params(("parallel",), 56),
        name="proj_prep",
    )(x, x, x, nw, w_main, w_ab, conv_w, neg_exp_alog, dt_bias)


def _stack(a, b):
    return jnp.concatenate([a.astype(BF16), b.astype(BF16)], axis=0)


def _merge_rows(direction, blk):
    first = blk if direction == 0 else 0
    return [slice(s, s + blk) for s in range(first, SCAN_CHUNK, 2 * blk)]


def _dn_scan_kernel(qf_ref, kf_ref, vf_ref, kTf_ref, gcf_ref, gTf_ref,
                    qb_ref, kb_ref, vb_ref, kTb_ref, gcb_ref, gTb_ref,
                    of_ref, ob_ref, S_ref):
    CH = SCAN_CHUNK
    NJ = qf_ref.shape[0] // CH
    refs = ((qf_ref, kf_ref, vf_ref, kTf_ref, gcf_ref, gTf_ref, of_ref),
            (qb_ref, kb_ref, vb_ref, kTb_ref, gcb_ref, gTb_ref, ob_ref))
    chains = [(d, j, h) for d in range(N_DIR) for j in range(NJ) for h in range(N_HEADS)]
    C = range(len(chains))
    tok = lambda j: slice(j * CH, (j + 1) * CH)

    @pl.when(pl.program_id(1) == 0)
    def _():
        S_ref[...] = jnp.zeros_like(S_ref)

    ri = lax.broadcasted_iota(jnp.int32, (CH, CH), 0)
    ci = lax.broadcasted_iota(jnp.int32, (CH, CH), 1)
    incl = (ri >= ci, ri <= ci)
    strict = (ri > ci, ri < ci)
    end = (CH - 1, 0)
    eye = jnp.where(ri == ci, 1.0, 0.0)
    same = lambda sh: jnp.right_shift(ri, sh) == jnp.right_shift(ci, sh)
    m16, m32, m64 = same(4), same(5), same(6)
    levels = ((16, m32 & ~m16), (32, m64 & ~m32), (64, ~m64))
    hs = lambda h: slice(h * HEAD_DIM, (h + 1) * HEAD_DIM)
    gc = [r[4][...] for r in refs]
    gT = [r[5][...] for r in refs]

    def col(d, j, h, base):
        l = base + d * N_HEADS + h
        return gc[d][tok(j), l:l + 1]

    Gc = [col(d, j, h, 0) for d, j, h in chains]
    beta = [col(d, j, h, N_DIR * N_HEADS) for d, j, h in chains]
    Gr = [gT[d][d * N_HEADS + h:d * N_HEADS + h + 1, tok(j)] for d, j, h in chains]
    G_end = [Gr[c][:, end[chains[c][0]]:end[chains[c][0]] + 1] for c in C]
    q = [refs[d][0][tok(j), hs(h)] for d, j, h in chains]
    k = [refs[d][1][tok(j), hs(h)] for d, j, h in chains]
    kT = [refs[d][3][hs(h), tok(j)] for d, j, h in chains]

    decay = [jnp.exp(jnp.where(incl[chains[c][0]], Gc[c] - Gr[c], NEG_BIG)) for c in C]
    KQ = [jnp.dot(jnp.concatenate([k[c], q[c]], axis=0), kT[c], preferred_element_type=F32) for c in C]
    A = [jnp.where(strict[chains[c][0]], beta[c] * KQ[c][:CH] * decay[c], 0.0) for c in C]
    qkm = [(KQ[c][CH:] * decay[c]).astype(BF16) for c in C]
    Dg = [jnp.where(m16, A[c], 0.0) for c in C]
    D2 = [_dot(Dg[c], Dg[c]) for c in C]
    X = [eye - Dg[c] for c in C]
    Y = [_dot(_stack(D2[c], X[c]), D2[c]) for c in C]
    D4 = [Y[c][:CH] for c in C]
    X = [X[c] + Y[c][CH:] for c in C]
    Y = [_dot(_stack(D4[c], X[c]), D4[c]) for c in C]
    X = [X[c] + Y[c][CH:] for c in C]
    X = [X[c] + _dot(X[c], Y[c][:CH]) for c in C]
    for blk, off in levels:
        rows = [_merge_rows(chains[c][0], blk) for c in C]
        Xr = [jnp.concatenate([X[c][r] for r in rows[c]], axis=0) for c in C]
        Z = [_dot(Xr[c], jnp.where(off, A[c], 0.0)) for c in C]
        Xr = [Xr[c] - _dot(Z[c], X[c]) for c in C]
        Xn = []
        for c in C:
            parts, pos = [], 0
            for j, r in enumerate(rows[c]):
                if r.start > pos:
                    parts.append(X[c][pos:r.start])
                parts.append(Xr[c][j * blk:(j + 1) * blk])
                pos = r.stop
            if pos < CH:
                parts.append(X[c][pos:CH])
            Xn.append(jnp.concatenate(parts, axis=0))
        X = Xn

    eG = [jnp.exp(Gc[c]) for c in C]
    kbe = [(k[c].astype(F32) * (beta[c] * eG[c])).astype(BF16) for c in C]
    vbeta = [refs[d][2][tok(j), hs(h)].astype(F32) * beta[c] for c, (d, j, h) in enumerate(chains)]
    qg = [(q[c].astype(F32) * eG[c]).astype(BF16) for c in C]
    keT = [(kT[c].astype(F32) * jnp.exp(G_end[c] - Gr[c])).astype(BF16) for c in C]
    dec = [jnp.exp(G_end[c]) for c in C]

    heads = [(d, h) for d in range(N_DIR) for h in range(N_HEADS)]
    Hd = range(len(heads))
    S = [S_ref[i] for i in Hd]
    for t in range(NJ):
        cur = [chains.index((d, t if d == 0 else NJ - 1 - t, h)) for d, h in heads]
        Sb = [S[i].astype(BF16) for i in Hd]
        kS = [jnp.dot(kbe[cur[i]], Sb[i], preferred_element_type=F32) for i in Hd]
        vnb = [_dot(X[cur[i]], vbeta[cur[i]] - kS[i]).astype(BF16) for i in Hd]
        o = [jnp.dot(jnp.concatenate([qg[cur[i]], qkm[cur[i]]], axis=1),
                     jnp.concatenate([Sb[i], vnb[i]], axis=0), preferred_element_type=F32) for i in Hd]
        dS = [jnp.dot(keT[cur[i]], vnb[i], preferred_element_type=F32) for i in Hd]
        for i in Hd:
            d, j, h = chains[cur[i]]
            refs[d][6][tok(j), hs(h)] = o[i].astype(refs[d][6].dtype)
        S = [S[i] * dec[cur[i]] + dS[i] for i in Hd]
    for i in Hd:
        S_ref[i] = S[i]


SCAN_BLOCK = 2 * SCAN_CHUNK


def _dn_scan(qn, kn, vc, knT, gcol, gT, seq_len):
    T = qn.shape[0]
    CH = min(SCAN_BLOCK, seq_len)
    nc = seq_len // CH
    nb = T // seq_len
    fwd = lambda b, c: b * nc + c
    bwd = lambda b, c: b * nc + (nc - 1 - c)

    def specs(blk):
        tok = pl.BlockSpec((CH, D_MODEL), lambda b, c: (blk(b, c), 0))
        return [tok, tok, tok,
                pl.BlockSpec((D_MODEL, CH), lambda b, c: (0, blk(b, c))),
                pl.BlockSpec((CH, 128), lambda b, c: (blk(b, c), 0)),
                pl.BlockSpec((N_DIR * N_HEADS, CH), lambda b, c: (0, blk(b, c)))]

    out = jax.ShapeDtypeStruct((T, D_MODEL), BF16)
    args = (qn, kn, vc, knT, gcol, gT)
    return pl.pallas_call(
        _dn_scan_kernel,
        grid=(nb, nc),
        in_specs=specs(fwd) + specs(bwd),
        out_specs=[specs(fwd)[0], specs(bwd)[0]],
        out_shape=[out, out],
        scratch_shapes=[pltpu.VMEM((N_DIR * N_HEADS, HEAD_DIM, HEAD_DIM), F32)],
        compiler_params=_params(("parallel", "arbitrary"), 32),
        name="dn_scan",
    )(*args, *args)


def _mix_kernel(u_ref, vs_ref, z_ref, ga_ref, gb_ref, of_ref, ob_ref, x_ref,
                lng_ref, lnb_ref, ws_ref, bs_ref, on_ref, wo_ref, out_ref, mix_ref):
    tm = x_ref.shape[0]
    v = _gelu(vs_ref[...].astype(F32))
    vcen = v - jnp.mean(v, axis=-1, keepdims=True)
    vln = vcen * lax.rsqrt(jnp.mean(vcen * vcen, axis=-1, keepdims=True) + EPS)
    vb = (vln * lng_ref[...] + lnb_ref[...]).astype(BF16)
    for g in range(SGU_GROUPS):
        gs = slice(g * HEAD_DIM, (g + 1) * HEAD_DIM)
        u = _gelu(u_ref[:, gs].astype(F32))
        gate_a = jax.nn.sigmoid(ga_ref[:, gs].astype(F32))
        o = of_ref[:, gs].astype(F32) + ob_ref[:, gs].astype(F32)
        o = o * lax.rsqrt(jnp.mean(o * o, axis=-1, keepdims=True) + EPS) * on_ref[...]
        o = o * _silu(z_ref[:, gs].astype(F32))
        mixed = jax.nn.sigmoid(gb_ref[:, gs].astype(F32)) * o
        w_sp = ws_ref[g]
        bias = bs_ref[:, g:g + 1]
        for c in range(tm // SGU_CHUNK):
            rs = slice(c * SGU_CHUNK, (c + 1) * SGU_CHUNK)
            s = jnp.dot(w_sp, vb[rs, gs], preferred_element_type=F32) + bias
            mix_ref[rs, gs] = (mixed[rs] + gate_a[rs] * (u[rs] * s)).astype(BF16)
    out_ref[...] = x_ref[...] + jnp.dot(mix_ref[...], wo_ref[...], preferred_element_type=F32)


def _mix(P, o_f, o_b, x, ln_g, ln_b, w_sp, b_spT, o_norm, w_out):
    T = x.shape[0]
    tm = min(256, T)
    col = lambda j: pl.BlockSpec((tm, D_MODEL), lambda i: (i, j))
    tok = pl.BlockSpec((tm, D_MODEL), lambda i: (i, 0))
    vec = pl.BlockSpec((1, D_MODEL), lambda i: (0, 0))
    return pl.pallas_call(
        _mix_kernel,
        grid=(T // tm,),
        in_specs=[
            col(0), col(1), col(2), col(3), col(4), tok, tok, tok, vec, vec,
            pl.BlockSpec((SGU_GROUPS, SGU_CHUNK, SGU_CHUNK), lambda i: (0, 0, 0)),
            pl.BlockSpec((SGU_CHUNK, SGU_GROUPS), lambda i: (0, 0)),
            pl.BlockSpec((1, HEAD_DIM), lambda i: (0, 0)),
            pl.BlockSpec((D_MODEL, D_MODEL), lambda i: (0, 0)),
        ],
        out_specs=tok,
        out_shape=jax.ShapeDtypeStruct((T, D_MODEL), F32),
        scratch_shapes=[pltpu.VMEM((tm, D_MODEL), BF16)],
        compiler_params=_params(("parallel",), 48),
        name="mix",
    )(P, P, P, P, P, o_f, o_b, x, ln_g, ln_b, w_sp, b_spT, o_norm, w_out)


FFN_BLOCK = 256


def _ffn_kernel(x_ref, nw_ref, wg_ref, wu_ref, wd_ref, o_ref):
    xf = x_ref[...]
    h = _rms(xf, nw_ref[...]).astype(BF16)
    o_ref[...] = xf
    for f in range(wg_ref.shape[1] // FFN_BLOCK):
        fs = slice(f * FFN_BLOCK, (f + 1) * FFN_BLOCK)
        g = jnp.dot(h, wg_ref[:, fs], preferred_element_type=F32)
        u = jnp.dot(h, wu_ref[:, fs], preferred_element_type=F32)
        o_ref[...] += jnp.dot((_silu(g) * u).astype(BF16), wd_ref[fs, :], preferred_element_type=F32)


def _ffn(x, nw, wg, wu, wd):
    T = x.shape[0]
    F = wg.shape[1]
    tm = min(512, T)
    tok = pl.BlockSpec((tm, D_MODEL), lambda i: (i, 0))
    resident = lambda shape: pl.BlockSpec(shape, lambda i: (0, 0), pipeline_mode=pl.Buffered(1))
    return pl.pallas_call(
        _ffn_kernel,
        grid=(T // tm,),
        in_specs=[
            tok,
            pl.BlockSpec((1, D_MODEL), lambda i: (0, 0)),
            resident((D_MODEL, F)), resident((D_MODEL, F)), resident((F, D_MODEL)),
        ],
        out_specs=tok,
        out_shape=jax.ShapeDtypeStruct((T, D_MODEL), F32),
        compiler_params=_params(("parallel",), 52),
        name="ffn",
    )(x, nw, wg, wu, wd)


def _moe_kernel(nf, ns, x_ref, nw_ref, wrh_ref, wrl_ref, wgu_ref, wd_ref, nfin_ref, o_ref,
                h_ref, rank_ref, gate_ref, cnt_ref, xe_ref, ye_ref):
    e = pl.program_id(1)
    f = pl.program_id(2)
    Ts = x_ref.shape[0] // ns
    nt = (((1,), (1,)), ((), ()))
    tn = (((0,), (0,)), ((), ()))

    @pl.when((e == 0) & (f == 0))
    def _route():
        ri = lax.broadcasted_iota(jnp.int32, (Ts, Ts), 0)
        ci = lax.broadcasted_iota(jnp.int32, (Ts, Ts), 1)
        before = jnp.where(ri < ci, 1.0, 0.0).astype(BF16)
        wrh = wrh_ref[...]
        for s in range(ns):
            tok = slice(s * Ts, (s + 1) * Ts)
            xf = x_ref[tok, :]
            h = _rms(xf, nw_ref[...])
            hb = h.astype(BF16)
            h_ref[tok, :] = hb
            hl = (h - hb.astype(F32)).astype(BF16)
            logits = (lax.dot_general(wrh, hb, nt, preferred_element_type=F32)
                      + lax.dot_general(wrh, hl, nt, preferred_element_type=F32)
                      + lax.dot_general(wrl_ref[...], hb, nt, preferred_element_type=F32))
            ei = lax.broadcasted_iota(jnp.int32, logits.shape, 0)
            m1 = jnp.max(logits, axis=0, keepdims=True)
            i1 = jnp.min(jnp.where(logits == m1, ei, N_EXPERTS), axis=0, keepdims=True)
            s1 = ei == i1
            rest = jnp.where(s1, -jnp.inf, logits)
            m2 = jnp.max(rest, axis=0, keepdims=True)
            i2 = jnp.min(jnp.where(rest == m2, ei, N_EXPERTS), axis=0, keepdims=True)
            s2 = ei == i2
            e2 = jnp.exp(m2 - m1)
            w1 = 1.0 / (1.0 + e2)
            gate_ref[s] = jnp.where(s1, w1, 0.0) + jnp.where(s2, e2 * w1, 0.0)
            sel = jnp.where(s1 | s2, 1.0, 0.0)
            rank = jnp.dot(sel.astype(BF16), before, preferred_element_type=F32)
            rank_ref[s] = jnp.where(sel > 0.0, rank, -1.0)
            cnt = jnp.sum(sel, axis=1, keepdims=True)
            for ee in range(N_EXPERTS):
                cnt_ref[s * N_EXPERTS + ee] = cnt[ee, 0].astype(jnp.int32)
            o_ref[tok, :] = xf

    for s in range(ns):
        tok = slice(s * Ts, (s + 1) * Ts)
        n_unit = (cnt_ref[s * N_EXPERTS + e] + MOE_UNIT - 1) // MOE_UNIT
        rem = n_unit % 3
        n_small = jnp.where(n_unit == 1, 1, jnp.where(rem == 0, 0, jnp.where(rem == 2, 1, 2)))
        n_large = (jnp.maximum(n_unit, 2) - 2 * n_small) // 3
        large_base = n_small * MOE_SMALL_BLOCK
        rk = rank_ref[s, pl.ds(e, 1), :]

        def for_blocks(fn, n_small=n_small, n_large=n_large, large_base=large_base):
            def small_body(j, carry):
                fn(pl.multiple_of(j * MOE_SMALL_BLOCK, MOE_SMALL_BLOCK), MOE_SMALL_BLOCK)
                return carry
            lax.fori_loop(0, n_small, small_body, 0)

            def large_body(j, carry):
                fn(pl.multiple_of(large_base + j * MOE_LARGE_BLOCK, MOE_UNIT), MOE_LARGE_BLOCK)
                return carry
            lax.fori_loop(0, n_large, large_body, 0)

        def one_hot(first, size, value, rk=rk):
            rows = (lax.broadcasted_iota(jnp.int32, (size, Ts), 0) + first).astype(F32)
            return jnp.where(rk == rows, value, 0.0).astype(BF16)

        @pl.when(f == 0)
        def _gather(s=s, tok=tok, for_blocks=for_blocks, one_hot=one_hot):
            def gather_block(first, size):
                xe_ref[s, pl.ds(first, size), :] = jnp.dot(
                    one_hot(first, size, 1.0), h_ref[tok, :], preferred_element_type=F32).astype(BF16)
                ye_ref[s, pl.ds(first, size), :] = jnp.zeros((size, D_MODEL), F32)
            for_blocks(gather_block)

        def expert_block(first, size, s=s):
            rows = pl.ds(first, size)
            xb = xe_ref[s, rows, :]
            gu = jnp.dot(xb, wgu_ref[0], preferred_element_type=F32)
            g, u = gu[:, :MOE_FFN_BLOCK], gu[:, MOE_FFN_BLOCK:]
            ye_ref[s, rows, :] += jnp.dot((_silu(g) * u).astype(BF16), wd_ref[0],
                                          preferred_element_type=F32)
        for_blocks(expert_block)

        @pl.when(f == nf - 1)
        def _scatter(s=s, tok=tok, for_blocks=for_blocks, one_hot=one_hot):
            gt = gate_ref[s, pl.ds(e, 1), :]

            def scatter_block(first, size):
                o_ref[tok, :] += lax.dot_general(
                    one_hot(first, size, gt), ye_ref[s, pl.ds(first, size), :].astype(BF16),
                    tn, preferred_element_type=F32)
            for_blocks(scatter_block)

    @pl.when((e == N_EXPERTS - 1) & (f == nf - 1))
    def _final():
        o_ref[...] = _rms(o_ref[...], nfin_ref[...])


def _moe(x, nw, wr_hi, wr_lo, wgu, wd, n_final):
    T = x.shape[0]
    tf = MOE_FFN_BLOCK
    nf = wd.shape[1] // tf
    ts = min(MOE_SUB_TILE, T)
    ns = min(MOE_SUB_TILES, T // ts)
    tm = ns * ts
    cap = max(-(-ts // MOE_UNIT) * MOE_UNIT, MOE_SMALL_BLOCK)
    tok = pl.BlockSpec((tm, D_MODEL), lambda i, e, f: (i, 0), pipeline_mode=pl.Buffered(1))
    vec = pl.BlockSpec((1, D_MODEL), lambda i, e, f: (0, 0))
    wr = pl.BlockSpec((N_EXPERTS, D_MODEL), lambda i, e, f: (0, 0))
    w_in = pl.BlockSpec((1, D_MODEL, 2 * tf), lambda i, e, f: (e, 0, f))
    return pl.pallas_call(
        functools.partial(_moe_kernel, nf, ns),
        grid=(T // tm, N_EXPERTS, nf),
        in_specs=[
            tok, vec, wr, wr, w_in,
            pl.BlockSpec((1, tf, D_MODEL), lambda i, e, f: (e, f, 0)),
            vec,
        ],
        out_specs=tok,
        out_shape=jax.ShapeDtypeStruct((T, D_MODEL), F32),
        scratch_shapes=[
            pltpu.VMEM((tm, D_MODEL), BF16),
            pltpu.VMEM((ns, N_EXPERTS, ts), F32),
            pltpu.VMEM((ns, N_EXPERTS, ts), F32),
            pltpu.SMEM((ns * N_EXPERTS,), jnp.int32),
            pltpu.VMEM((ns, cap, D_MODEL), BF16),
            pltpu.VMEM((ns, cap, D_MODEL), F32),
        ],
        compiler_params=_params(("parallel", "arbitrary", "arbitrary"), 58),
        name="moe",
    )(x, nw, wr_hi, wr_lo, wgu, wd, n_final)


def _pad_lanes(v, width=128):
    v = v.reshape(1, -1).astype(F32)
    return jnp.pad(v, ((0, 0), (0, width - v.shape[1])))


def _prepare(norm_mix, w_in, sgu_ln_gain, sgu_ln_bias, sgu_w_spatial, sgu_b_spatial,
             dn_conv_w, dn_a_log, dn_dt_bias, dn_out_norm, w_out, norm_ffn,
             ffn_w_gate, ffn_w_up, ffn_w_down, moe_w_router, moe_w_gate, moe_w_up,
             moe_w_down, norm_final):
    depth = w_in.shape[0]
    W = D_MODEL
    layers = []
    for i in range(depth):
        wi = w_in[i]
        seg = lambda a, b: wi[:, a:b]
        ab0 = 6 * W
        ab1 = ab0 + 2 * N_DIR * N_HEADS
        w_main = jnp.concatenate(
            [seg(2 * W, 5 * W), seg(0, 2 * W), seg(5 * W, 6 * W), seg(ab1, ab1 + 2 * W)],
            axis=1).astype(BF16)
        w_ab = jnp.pad(seg(ab0, ab1), ((0, 0), (0, 128 - (ab1 - ab0)))).astype(BF16)
        lp = dict(
            norm_mix=norm_mix[i].reshape(1, W), w_main=w_main, w_ab=w_ab,
            ln_g=sgu_ln_gain[i].reshape(1, W), ln_b=sgu_ln_bias[i].reshape(1, W),
            w_sp=sgu_w_spatial[i].astype(BF16), b_spT=jnp.transpose(sgu_b_spatial[i]),
            conv_w=dn_conv_w[i],
            neg_exp_alog=_pad_lanes(-jnp.exp(dn_a_log[i].astype(F32))),
            dt_bias=_pad_lanes(dn_dt_bias[i]),
            o_norm=dn_out_norm[i].reshape(1, HEAD_DIM), w_out=w_out[i].astype(BF16),
            norm_ffn=norm_ffn[i].reshape(1, W),
        )
        j = i // 2
        if i % 2 == 0:
            lp.update(wg=ffn_w_gate[j].astype(BF16), wu=ffn_w_up[j].astype(BF16),
                      wd=ffn_w_down[j].astype(BF16))
        else:
            wrT = jnp.transpose(moe_w_router[j]).astype(F32)
            wr_hi = wrT.astype(BF16)
            tf = MOE_FFN_BLOCK
            wgu = jnp.concatenate(
                [w[:, :, f * tf:(f + 1) * tf].astype(BF16)
                 for f in range(moe_w_gate.shape[-1] // tf) for w in (moe_w_gate[j], moe_w_up[j])],
                axis=-1)
            lp.update(wr_hi=wr_hi, wr_lo=(wrT - wr_hi.astype(F32)).astype(BF16),
                      wgu=wgu, wd=moe_w_down[j].astype(BF16))
        layers.append(lp)
    return layers, norm_final.reshape(1, W)


def _trunk(x3, layers, n_final):
    B, L, W = x3.shape
    x = x3.reshape(B * L, W)
    depth = len(layers)
    assert depth % 2 == 0, "the final RMSNorm is fused into the last (expert) layer"
    for i, lp in enumerate(layers):
        P, qn, kn, vc, knT, gcol, gT = _proj_prep(
            x, lp["norm_mix"], lp["w_main"], lp["w_ab"], lp["conv_w"], lp["neg_exp_alog"],
            lp["dt_bias"], L)
        o_f, o_b = _dn_scan(qn, kn, vc, knT, gcol, gT, L)
        x = _mix(P, o_f, o_b, x, lp["ln_g"], lp["ln_b"], lp["w_sp"], lp["b_spT"],
                 lp["o_norm"], lp["w_out"])
        if i % 2 == 0:
            x = _ffn(x, lp["norm_ffn"], lp["wg"], lp["wu"], lp["wd"])
        else:
            assert i == depth - 1
            x = _moe(x, lp["norm_ffn"], lp["wr_hi"], lp["wr_lo"], lp["wgu"], lp["wd"], n_final)
    return x.reshape(B, L, W)


def kernel(x_prompt, x_sample, norm_mix, w_in, sgu_ln_gain, sgu_ln_bias, sgu_w_spatial, sgu_b_spatial, dn_conv_w, dn_a_log, dn_dt_bias, dn_out_norm, w_out, norm_ffn, ffn_w_gate, ffn_w_up, ffn_w_down, moe_w_router, moe_w_gate, moe_w_up, moe_w_down, norm_final):
    layers, n_final = _prepare(norm_mix, w_in, sgu_ln_gain, sgu_ln_bias, sgu_w_spatial,
                               sgu_b_spatial, dn_conv_w, dn_a_log, dn_dt_bias, dn_out_norm,
                               w_out, norm_ffn, ffn_w_gate, ffn_w_up, ffn_w_down, moe_w_router,
                               moe_w_gate, moe_w_up, moe_w_down, norm_final)
    return (_trunk(x_prompt, layers, n_final), _trunk(x_sample, layers, n_final))
```

```python
import functools

import jax
import jax.numpy as jnp
from jax import lax
from jax.experimental import pallas as pl
from jax.experimental.pallas import tpu as pltpu

F32 = jnp.float32
BF16 = jnp.bfloat16
EPS = 1e-6

D_MODEL = 1024
N_HEADS = 8
HEAD_DIM = 128
N_DIR = 2
CONV_K = 5
CONV_HALO = 8
HALO_BLOCK = 16
SGU_CHUNK = 128
SGU_GROUPS = 8
N_EXPERTS = 8
SCAN_CHUNK = 128
MOE_UNIT = 96
MOE_SMALL_BLOCK = 2 * MOE_UNIT
MOE_LARGE_BLOCK = 3 * MOE_UNIT
MOE_SUB_TILE = 1024
MOE_SUB_TILES = 2
MOE_FFN_BLOCK = 896
NEG_BIG = -1e30


def _params(sem, vmem_mb):
    return pltpu.CompilerParams(dimension_semantics=sem, vmem_limit_bytes=vmem_mb * 2**20)


def _dot(a, b):
    return jnp.dot(a.astype(BF16), b.astype(BF16), preferred_element_type=F32)


def _rms(xf, w):
    return xf * lax.rsqrt(jnp.mean(xf * xf, axis=-1, keepdims=True) + EPS) * w


def _silu(x):
    return x * jax.nn.sigmoid(x)


def _gelu(x):
    return 0.5 * x * (1.0 + lax.erf(x * (2.0 ** -0.5)))


def _in_proj_kernel(x_ref, nw_ref, w_ref, wab_ref, p_ref, ab_ref):
    hb = _rms(x_ref[...], nw_ref[...]).astype(BF16)
    ab_ref[...] = jnp.dot(hb, wab_ref[...], preferred_element_type=F32)
    for j in range(w_ref.shape[1] // D_MODEL):
        cs = slice(j * D_MODEL, (j + 1) * D_MODEL)
        p_ref[:, cs] = jnp.dot(hb, w_ref[:, cs], preferred_element_type=F32).astype(BF16)


def _in_proj(x, nw, w_main, w_ab):
    T = x.shape[0]
    tm = min(512, T)
    n_cols = w_main.shape[1]
    resident = lambda shape: pl.BlockSpec(shape, lambda i: (0, 0), pipeline_mode=pl.Buffered(1))
    return pl.pallas_call(
        _in_proj_kernel,
        grid=(T // tm,),
        in_specs=[
            pl.BlockSpec((tm, D_MODEL), lambda i: (i, 0)),
            pl.BlockSpec((1, D_MODEL), lambda i: (0, 0)),
            resident((D_MODEL, n_cols)),
            resident((D_MODEL, 128)),
        ],
        out_specs=[
            pl.BlockSpec((tm, n_cols), lambda i: (i, 0)),
            pl.BlockSpec((tm, 128), lambda i: (i, 0)),
        ],
        out_shape=[
            jax.ShapeDtypeStruct((T, n_cols), BF16),
            jax.ShapeDtypeStruct((T, 128), F32),
        ],
        compiler_params=_params(("parallel",), 52),
        name="in_proj",
    )(x, nw, w_main, w_ab)


def _dn_prep_kernel(tiles_per_seq, pm_ref, pp_ref, pn_ref, ab_ref, cw_ref, nea_ref, dtb_ref,
                    qn_ref, kn_ref, vc_ref, knT_ref, gcol_ref, gT_ref, xp_ref):
    tm = pm_ref.shape[0]
    pos = pl.program_id(0) % tiles_per_seq
    keep_prev = jnp.where(pos == 0, 0.0, 1.0)
    keep_next = jnp.where(pos == tiles_per_seq - 1, 0.0, 1.0)
    q_scale = HEAD_DIM ** -0.5
    half = CONV_K // 2
    E = CONV_HALO
    ri = lax.broadcasted_iota(jnp.int32, (tm, tm), 0)
    ci = lax.broadcasted_iota(jnp.int32, (tm, tm), 1)
    offsets = [d for d in range(-half, half + 1) if d != 0]
    shift_mat = jnp.concatenate([jnp.where(ci == ri + d, 1.0, 0.0).astype(BF16) for d in offsets],
                                axis=0)

    for s in range(3):
        cs = slice(s * D_MODEL, (s + 1) * D_MODEL)
        xb = pm_ref[:, cs]
        xf = xb.astype(F32)
        taps = [cw_ref[k:k + 1, cs] for k in range(CONV_K)]
        shifted = jnp.dot(shift_mat, xb, preferred_element_type=F32)
        acc = None
        for k in range(CONV_K):
            d = k - half
            src = xf if d == 0 else shifted[offsets.index(d) * tm:(offsets.index(d) + 1) * tm]
            acc = src * taps[k] if acc is None else acc + src * taps[k]
        xp_ref[0, 0:E, :] = pp_ref[:, cs].astype(F32)[HALO_BLOCK - E:, :] * keep_prev
        xp_ref[0, E:3 * E, :] = xf[0:2 * E]
        xp_ref[1, 0:2 * E, :] = xf[tm - 2 * E:tm]
        xp_ref[1, 2 * E:3 * E, :] = pn_ref[:, cs].astype(F32)[:E, :] * keep_next
        edges = []
        for side in range(2):
            e_acc = None
            for k in range(CONV_K):
                start = E - half + k
                term = xp_ref[side, start:start + E, :] * taps[k]
                e_acc = term if e_acc is None else e_acc + term
            edges.append(e_acc)
        y = _silu(jnp.concatenate([edges[0], acc[E:tm - E], edges[1]], axis=0))
        if s == 2:
            vc_ref[...] = y.astype(BF16)
            continue
        for h in range(N_HEADS):
            hs = slice(h * HEAD_DIM, (h + 1) * HEAD_DIM)
            slab = y[:, hs]
            slab = slab * lax.rsqrt(jnp.sum(slab * slab, axis=-1, keepdims=True) + EPS)
            if s == 0:
                qn_ref[:, hs] = (slab * q_scale).astype(BF16)
            else:
                kn_ref[:, hs] = slab.astype(BF16)
                knT_ref[hs, :] = jnp.transpose(slab).astype(BF16)

    ab = ab_ref[...]
    z = ab + dtb_ref[...]
    softplus = jnp.maximum(z, 0.0) + jnp.log(1.0 + jnp.exp(-jnp.abs(z)))
    g = nea_ref[...] * softplus
    beta = jax.nn.sigmoid(ab)
    ri = lax.broadcasted_iota(jnp.int32, (tm, tm), 0)
    ci = lax.broadcasted_iota(jnp.int32, (tm, tm), 1)
    same = jnp.right_shift(ri, 7) == jnp.right_shift(ci, 7)
    lower = jnp.where(same & (ci <= ri), 1.0, 0.0).astype(BF16)
    upper = jnp.where(same & (ci >= ri), 1.0, 0.0).astype(BF16)
    g_hi = g.astype(BF16)
    g_lo = (g - g_hi.astype(F32)).astype(BF16)
    g_fwd = (jnp.dot(lower, g_hi, preferred_element_type=F32)
             + jnp.dot(lower, g_lo, preferred_element_type=F32))
    g_bwd = (jnp.dot(upper, g_hi, preferred_element_type=F32)
             + jnp.dot(upper, g_lo, preferred_element_type=F32))
    lane = lax.broadcasted_iota(jnp.int32, g.shape, 1)
    gcol = jnp.where(lane < N_HEADS, g_fwd, jnp.where(lane < N_DIR * N_HEADS, g_bwd, beta))
    gcol_ref[...] = gcol
    gT_ref[...] = jnp.transpose(gcol)[0:N_DIR * N_HEADS, :]


def _dn_prep(P, ab, conv_w, neg_exp_alog, dt_bias, seq_len):
    T = P.shape[0]
    tm = min(256, seq_len)
    tiles_per_seq = seq_len // tm
    hb = tm // HALO_BLOCK
    n_hb = T // HALO_BLOCK
    qkv_w = 3 * D_MODEL
    tok = jax.ShapeDtypeStruct((T, D_MODEL), BF16)
    tok_spec = pl.BlockSpec((tm, D_MODEL), lambda i: (i, 0))
    return pl.pallas_call(
        functools.partial(_dn_prep_kernel, tiles_per_seq),
        grid=(T // tm,),
        in_specs=[
            pl.BlockSpec((tm, qkv_w), lambda i: (i, 0)),
            pl.BlockSpec((HALO_BLOCK, qkv_w), lambda i: (jnp.maximum(i * hb - 1, 0), 0)),
            pl.BlockSpec((HALO_BLOCK, qkv_w), lambda i: (jnp.minimum((i + 1) * hb, n_hb - 1), 0)),
            pl.BlockSpec((tm, 128), lambda i: (i, 0)),
            pl.BlockSpec((CONV_K, qkv_w), lambda i: (0, 0)),
            pl.BlockSpec((1, 128), lambda i: (0, 0)),
            pl.BlockSpec((1, 128), lambda i: (0, 0)),
        ],
        out_specs=[
            tok_spec, tok_spec, tok_spec,
            pl.BlockSpec((D_MODEL, tm), lambda i: (0, i)),
            pl.BlockSpec((tm, 128), lambda i: (i, 0)),
            pl.BlockSpec((N_DIR * N_HEADS, tm), lambda i: (0, i)),
        ],
        out_shape=[tok, tok, tok,
                   jax.ShapeDtypeStruct((D_MODEL, T), BF16),
                   jax.ShapeDtypeStruct((T, 128), F32),
                   jax.ShapeDtypeStruct((N_DIR * N_HEADS, T), F32)],
        scratch_shapes=[pltpu.VMEM((2, 3 * CONV_HALO, D_MODEL), F32)],
        compiler_params=_params(("parallel",), 48),
        name="dn_prep",
    )(P, P, P, ab, conv_w, neg_exp_alog, dt_bias)


def _stack(a, b):
    return jnp.concatenate([a.astype(BF16), b.astype(BF16)], axis=0)


def _merge_rows(direction, blk):
    first = blk if direction == 0 else 0
    return [slice(s, s + blk) for s in range(first, SCAN_CHUNK, 2 * blk)]


def _dn_scan_kernel(qf_ref, kf_ref, vf_ref, kTf_ref, gcf_ref, gTf_ref,
                    qb_ref, kb_ref, vb_ref, kTb_ref, gcb_ref, gTb_ref,
                    of_ref, ob_ref, S_ref):
    CH = SCAN_CHUNK
    NJ = qf_ref.shape[0] // CH
    refs = ((qf_ref, kf_ref, vf_ref, kTf_ref, gcf_ref, gTf_ref, of_ref),
            (qb_ref, kb_ref, vb_ref, kTb_ref, gcb_ref, gTb_ref, ob_ref))
    chains = [(d, j, h) for d in range(N_DIR) for j in range(NJ) for h in range(N_HEADS)]
    C = range(len(chains))
    tok = lambda j: slice(j * CH, (j + 1) * CH)

    @pl.when(pl.program_id(1) == 0)
    def _():
        S_ref[...] = jnp.zeros_like(S_ref)

    ri = lax.broadcasted_iota(jnp.int32, (CH, CH), 0)
    ci = lax.broadcasted_iota(jnp.int32, (CH, CH), 1)
    incl = (ri >= ci, ri <= ci)
    strict = (ri > ci, ri < ci)
    end = (CH - 1, 0)
    eye = jnp.where(ri == ci, 1.0, 0.0)
    same = lambda sh: jnp.right_shift(ri, sh) == jnp.right_shift(ci, sh)
    m16, m32, m64 = same(4), same(5), same(6)
    levels = ((16, m32 & ~m16), (32, m64 & ~m32), (64, ~m64))
    hs = lambda h: slice(h * HEAD_DIM, (h + 1) * HEAD_DIM)
    gc = [r[4][...] for r in refs]
    gT = [r[5][...] for r in refs]

    def col(d, j, h, base):
        l = base + d * N_HEADS + h
        return gc[d][tok(j), l:l + 1]

    Gc = [col(d, j, h, 0) for d, j, h in chains]
    beta = [col(d, j, h, N_DIR * N_HEADS) for d, j, h in chains]
    Gr = [gT[d][d * N_HEADS + h:d * N_HEADS + h + 1, tok(j)] for d, j, h in chains]
    G_end = [Gr[c][:, end[chains[c][0]]:end[chains[c][0]] + 1] for c in C]
    q = [refs[d][0][tok(j), hs(h)] for d, j, h in chains]
    k = [refs[d][1][tok(j), hs(h)] for d, j, h in chains]
    kT = [refs[d][3][hs(h), tok(j)] for d, j, h in chains]

    decay = [jnp.exp(jnp.where(incl[chains[c][0]], Gc[c] - Gr[c], NEG_BIG)) for c in C]
    KQ = [jnp.dot(jnp.concatenate([k[c], q[c]], axis=0), kT[c], preferred_element_type=F32) for c in C]
    A = [jnp.where(strict[chains[c][0]], beta[c] * KQ[c][:CH] * decay[c], 0.0) for c in C]
    qkm = [(KQ[c][CH:] * decay[c]).astype(BF16) for c in C]
    Dg = [jnp.where(m16, A[c], 0.0) for c in C]
    D2 = [_dot(Dg[c], Dg[c]) for c in C]
    X = [eye - Dg[c] for c in C]
    Y = [_dot(_stack(D2[c], X[c]), D2[c]) for c in C]
    D4 = [Y[c][:CH] for c in C]
    X = [X[c] + Y[c][CH:] for c in C]
    Y = [_dot(_stack(D4[c], X[c]), D4[c]) for c in C]
    X = [X[c] + Y[c][CH:] for c in C]
    X = [X[c] + _dot(X[c], Y[c][:CH]) for c in C]
    for blk, off in levels:
        rows = [_merge_rows(chains[c][0], blk) for c in C]
        Xr = [jnp.concatenate([X[c][r] for r in rows[c]], axis=0) for c in C]
        Z = [_dot(Xr[c], jnp.where(off, A[c], 0.0)) for c in C]
        Xr = [Xr[c] - _dot(Z[c], X[c]) for c in C]
        Xn = []
        for c in C:
            parts, pos = [], 0
            for j, r in enumerate(rows[c]):
                if r.start > pos:
                    parts.append(X[c][pos:r.start])
                parts.append(Xr[c][j * blk:(j + 1) * blk])
                pos = r.stop
            if pos < CH:
                parts.append(X[c][pos:CH])
            Xn.append(jnp.concatenate(parts, axis=0))
        X = Xn

    eG = [jnp.exp(Gc[c]) for c in C]
    kbe = [(k[c].astype(F32) * (beta[c] * eG[c])).astype(BF16) for c in C]
    vbeta = [refs[d][2][tok(j), hs(h)].astype(F32) * beta[c] for c, (d, j, h) in enumerate(chains)]
    qg = [(q[c].astype(F32) * eG[c]).astype(BF16) for c in C]
    keT = [(kT[c].astype(F32) * jnp.exp(G_end[c] - Gr[c])).astype(BF16) for c in C]
    dec = [jnp.exp(G_end[c]) for c in C]

    heads = [(d, h) for d in range(N_DIR) for h in range(N_HEADS)]
    Hd = range(len(heads))
    S = [S_ref[i] for i in Hd]
    for t in range(NJ):
        cur = [chains.index((d, t if d == 0 else NJ - 1 - t, h)) for d, h in heads]
        Sb = [S[i].astype(BF16) for i in Hd]
        kS = [jnp.dot(kbe[cur[i]], Sb[i], preferred_element_type=F32) for i in Hd]
        vnb = [_dot(X[cur[i]], vbeta[cur[i]] - kS[i]).astype(BF16) for i in Hd]
        o = [jnp.dot(jnp.concatenate([qg[cur[i]], qkm[cur[i]]], axis=1),
                     jnp.concatenate([Sb[i], vnb[i]], axis=0), preferred_element_type=F32) for i in Hd]
        dS = [jnp.dot(keT[cur[i]], vnb[i], preferred_element_type=F32) for i in Hd]
        for i in Hd:
            d, j, h = chains[cur[i]]
            refs[d][6][tok(j), hs(h)] = o[i].astype(refs[d][6].dtype)
        S = [S[i] * dec[cur[i]] + dS[i] for i in Hd]
    for i in Hd:
        S_ref[i] = S[i]


SCAN_BLOCK = 2 * SCAN_CHUNK


def _dn_scan(qn, kn, vc, knT, gcol, gT, seq_len):
    T = qn.shape[0]
    CH = min(SCAN_BLOCK, seq_len)
    nc = seq_len // CH
    nb = T // seq_len
    fwd = lambda b, c: b * nc + c
    bwd = lambda b, c: b * nc + (nc - 1 - c)

    def specs(blk):
        tok = pl.BlockSpec((CH, D_MODEL), lambda b, c: (blk(b, c), 0))
        return [tok, tok, tok,
                pl.BlockSpec((D_MODEL, CH), lambda b, c: (0, blk(b, c))),
                pl.BlockSpec((CH, 128), lambda b, c: (blk(b, c), 0)),
                pl.BlockSpec((N_DIR * N_HEADS, CH), lambda b, c: (0, blk(b, c)))]

    out = jax.ShapeDtypeStruct((T, D_MODEL), BF16)
    args = (qn, kn, vc, knT, gcol, gT)
    return pl.pallas_call(
        _dn_scan_kernel,
        grid=(nb, nc),
        in_specs=specs(fwd) + specs(bwd),
        out_specs=[specs(fwd)[0], specs(bwd)[0]],
        out_shape=[out, out],
        scratch_shapes=[pltpu.VMEM((N_DIR * N_HEADS, HEAD_DIM, HEAD_DIM), F32)],
        compiler_params=_params(("parallel", "arbitrary"), 32),
        name="dn_scan",
    )(*args, *args)


def _mix_kernel(u_ref, vs_ref, z_ref, ga_ref, gb_ref, of_ref, ob_ref, x_ref,
                lng_ref, lnb_ref, ws_ref, bs_ref, on_ref, wo_ref, out_ref, mix_ref):
    tm = x_ref.shape[0]
    v = _gelu(vs_ref[...].astype(F32))
    vcen = v - jnp.mean(v, axis=-1, keepdims=True)
    vln = vcen * lax.rsqrt(jnp.mean(vcen * vcen, axis=-1, keepdims=True) + EPS)
    vb = (vln * lng_ref[...] + lnb_ref[...]).astype(BF16)
    for g in range(SGU_GROUPS):
        gs = slice(g * HEAD_DIM, (g + 1) * HEAD_DIM)
        u = _gelu(u_ref[:, gs].astype(F32))
        gate_a = jax.nn.sigmoid(ga_ref[:, gs].astype(F32))
        o = of_ref[:, gs].astype(F32) + ob_ref[:, gs].astype(F32)
        o = o * lax.rsqrt(jnp.mean(o * o, axis=-1, keepdims=True) + EPS) * on_ref[...]
        o = o * _silu(z_ref[:, gs].astype(F32))
        mixed = jax.nn.sigmoid(gb_ref[:, gs].astype(F32)) * o
        w_sp = ws_ref[g]
        bias = bs_ref[:, g:g + 1]
        for c in range(tm // SGU_CHUNK):
            rs = slice(c * SGU_CHUNK, (c + 1) * SGU_CHUNK)
            s = jnp.dot(w_sp, vb[rs, gs], preferred_element_type=F32) + bias
            mix_ref[rs, gs] = (mixed[rs] + gate_a[rs] * (u[rs] * s)).astype(BF16)
    out_ref[...] = x_ref[...] + jnp.dot(mix_ref[...], wo_ref[...], preferred_element_type=F32)


def _mix(P, o_f, o_b, x, ln_g, ln_b, w_sp, b_spT, o_norm, w_out):
    T = x.shape[0]
    tm = min(256, T)
    col = lambda j: pl.BlockSpec((tm, D_MODEL), lambda i: (i, j))
    tok = pl.BlockSpec((tm, D_MODEL), lambda i: (i, 0))
    vec = pl.BlockSpec((1, D_MODEL), lambda i: (0, 0))
    return pl.pallas_call(
        _mix_kernel,
        grid=(T // tm,),
        in_specs=[
            col(3), col(4), col(5), col(6), col(7), tok, tok, tok, vec, vec,
            pl.BlockSpec((SGU_GROUPS, SGU_CHUNK, SGU_CHUNK), lambda i: (0, 0, 0)),
            pl.BlockSpec((SGU_CHUNK, SGU_GROUPS), lambda i: (0, 0)),
            pl.BlockSpec((1, HEAD_DIM), lambda i: (0, 0)),
            pl.BlockSpec((D_MODEL, D_MODEL), lambda i: (0, 0)),
        ],
        out_specs=tok,
        out_shape=jax.ShapeDtypeStruct((T, D_MODEL), F32),
        scratch_shapes=[pltpu.VMEM((tm, D_MODEL), BF16)],
        compiler_params=_params(("parallel",), 48),
        name="mix",
    )(P, P, P, P, P, o_f, o_b, x, ln_g, ln_b, w_sp, b_spT, o_norm, w_out)


FFN_BLOCK = 256


def _ffn_kernel(x_ref, nw_ref, wg_ref, wu_ref, wd_ref, o_ref):
    xf = x_ref[...]
    h = _rms(xf, nw_ref[...]).astype(BF16)
    o_ref[...] = xf
    for f in range(wg_ref.shape[1] // FFN_BLOCK):
        fs = slice(f * FFN_BLOCK, (f + 1) * FFN_BLOCK)
        g = jnp.dot(h, wg_ref[:, fs], preferred_element_type=F32)
        u = jnp.dot(h, wu_ref[:, fs], preferred_element_type=F32)
        o_ref[...] += jnp.dot((_silu(g) * u).astype(BF16), wd_ref[fs, :], preferred_element_type=F32)


def _ffn(x, nw, wg, wu, wd):
    T = x.shape[0]
    F = wg.shape[1]
    tm = min(512, T)
    tok = pl.BlockSpec((tm, D_MODEL), lambda i: (i, 0))
    resident = lambda shape: pl.BlockSpec(shape, lambda i: (0, 0), pipeline_mode=pl.Buffered(1))
    return pl.pallas_call(
        _ffn_kernel,
        grid=(T // tm,),
        in_specs=[
            tok,
            pl.BlockSpec((1, D_MODEL), lambda i: (0, 0)),
            resident((D_MODEL, F)), resident((D_MODEL, F)), resident((F, D_MODEL)),
        ],
        out_specs=tok,
        out_shape=jax.ShapeDtypeStruct((T, D_MODEL), F32),
        compiler_params=_params(("parallel",), 52),
        name="ffn",
    )(x, nw, wg, wu, wd)


def _moe_kernel(nf, ns, x_ref, nw_ref, wrh_ref, wrl_ref, wgu_ref, wd_ref, nfin_ref, o_ref,
                h_ref, rank_ref, gate_ref, cnt_ref, xe_ref, ye_ref):
    e = pl.program_id(1)
    f = pl.program_id(2)
    Ts = x_ref.shape[0] // ns
    nt = (((1,), (1,)), ((), ()))
    tn = (((0,), (0,)), ((), ()))

    @pl.when((e == 0) & (f == 0))
    def _route():
        ri = lax.broadcasted_iota(jnp.int32, (Ts, Ts), 0)
        ci = lax.broadcasted_iota(jnp.int32, (Ts, Ts), 1)
        before = jnp.where(ri < ci, 1.0, 0.0).astype(BF16)
        wrh = wrh_ref[...]
        for s in range(ns):
            tok = slice(s * Ts, (s + 1) * Ts)
            xf = x_ref[tok, :]
            h = _rms(xf, nw_ref[...])
            hb = h.astype(BF16)
            h_ref[tok, :] = hb
            hl = (h - hb.astype(F32)).astype(BF16)
            logits = (lax.dot_general(wrh, hb, nt, preferred_element_type=F32)
                      + lax.dot_general(wrh, hl, nt, preferred_element_type=F32)
                      + lax.dot_general(wrl_ref[...], hb, nt, preferred_element_type=F32))
            ei = lax.broadcasted_iota(jnp.int32, logits.shape, 0)
            m1 = jnp.max(logits, axis=0, keepdims=True)
            i1 = jnp.min(jnp.where(logits == m1, ei, N_EXPERTS), axis=0, keepdims=True)
            s1 = ei == i1
            rest = jnp.where(s1, -jnp.inf, logits)
            m2 = jnp.max(rest, axis=0, keepdims=True)
            i2 = jnp.min(jnp.where(rest == m2, ei, N_EXPERTS), axis=0, keepdims=True)
            s2 = ei == i2
            e2 = jnp.exp(m2 - m1)
            w1 = 1.0 / (1.0 + e2)
            gate_ref[s] = jnp.where(s1, w1, 0.0) + jnp.where(s2, e2 * w1, 0.0)
            sel = jnp.where(s1 | s2, 1.0, 0.0)
            rank = jnp.dot(sel.astype(BF16), before, preferred_element_type=F32)
            rank_ref[s] = jnp.where(sel > 0.0, rank, -1.0)
            cnt = jnp.sum(sel, axis=1, keepdims=True)
            for ee in range(N_EXPERTS):
                cnt_ref[s * N_EXPERTS + ee] = cnt[ee, 0].astype(jnp.int32)
            o_ref[tok, :] = xf

    for s in range(ns):
        tok = slice(s * Ts, (s + 1) * Ts)
        n_unit = (cnt_ref[s * N_EXPERTS + e] + MOE_UNIT - 1) // MOE_UNIT
        rem = n_unit % 3
        n_small = jnp.where(n_unit == 1, 1, jnp.where(rem == 0, 0, jnp.where(rem == 2, 1, 2)))
        n_large = (jnp.maximum(n_unit, 2) - 2 * n_small) // 3
        large_base = n_small * MOE_SMALL_BLOCK
        rk = rank_ref[s, pl.ds(e, 1), :]

        def for_blocks(fn, n_small=n_small, n_large=n_large, large_base=large_base):
            def small_body(j, carry):
                fn(pl.multiple_of(j * MOE_SMALL_BLOCK, MOE_SMALL_BLOCK), MOE_SMALL_BLOCK)
                return carry
            lax.fori_loop(0, n_small, small_body, 0)

            def large_body(j, carry):
                fn(pl.multiple_of(large_base + j * MOE_LARGE_BLOCK, MOE_UNIT), MOE_LARGE_BLOCK)
                return carry
            lax.fori_loop(0, n_large, large_body, 0)

        def one_hot(first, size, value, rk=rk):
            rows = (lax.broadcasted_iota(jnp.int32, (size, Ts), 0) + first).astype(F32)
            return jnp.where(rk == rows, value, 0.0).astype(BF16)

        @pl.when(f == 0)
        def _gather(s=s, tok=tok, for_blocks=for_blocks, one_hot=one_hot):
            def gather_block(first, size):
                xe_ref[s, pl.ds(first, size), :] = jnp.dot(
                    one_hot(first, size, 1.0), h_ref[tok, :], preferred_element_type=F32).astype(BF16)
                ye_ref[s, pl.ds(first, size), :] = jnp.zeros((size, D_MODEL), F32)
            for_blocks(gather_block)

        def expert_block(first, size, s=s):
            rows = pl.ds(first, size)
            xb = xe_ref[s, rows, :]
            gu = jnp.dot(xb, wgu_ref[0], preferred_element_type=F32)
            g, u = gu[:, :MOE_FFN_BLOCK], gu[:, MOE_FFN_BLOCK:]
            ye_ref[s, rows, :] += jnp.dot((_silu(g) * u).astype(BF16), wd_ref[0],
                                          preferred_element_type=F32)
        for_blocks(expert_block)

        @pl.when(f == nf - 1)
        def _scatter(s=s, tok=tok, for_blocks=for_blocks, one_hot=one_hot):
            gt = gate_ref[s, pl.ds(e, 1), :]

            def scatter_block(first, size):
                o_ref[tok, :] += lax.dot_general(
                    one_hot(first, size, gt), ye_ref[s, pl.ds(first, size), :].astype(BF16),
                    tn, preferred_element_type=F32)
            for_blocks(scatter_block)

    @pl.when((e == N_EXPERTS - 1) & (f == nf - 1))
    def _final():
        o_ref[...] = _rms(o_ref[...], nfin_ref[...])


def _moe(x, nw, wr_hi, wr_lo, wgu, wd, n_final):
    T = x.shape[0]
    tf = MOE_FFN_BLOCK
    nf = wd.shape[1] // tf
    ts = min(MOE_SUB_TILE, T)
    ns = min(MOE_SUB_TILES, T // ts)
    tm = ns * ts
    cap = max(-(-ts // MOE_UNIT) * MOE_UNIT, MOE_SMALL_BLOCK)
    tok = pl.BlockSpec((tm, D_MODEL), lambda i, e, f: (i, 0), pipeline_mode=pl.Buffered(1))
    vec = pl.BlockSpec((1, D_MODEL), lambda i, e, f: (0, 0))
    wr = pl.BlockSpec((N_EXPERTS, D_MODEL), lambda i, e, f: (0, 0))
    w_in = pl.BlockSpec((1, D_MODEL, 2 * tf), lambda i, e, f: (e, 0, f))
    return pl.pallas_call(
        functools.partial(_moe_kernel, nf, ns),
        grid=(T // tm, N_EXPERTS, nf),
        in_specs=[
            tok, vec, wr, wr, w_in,
            pl.BlockSpec((1, tf, D_MODEL), lambda i, e, f: (e, f, 0)),
            vec,
        ],
        out_specs=tok,
        out_shape=jax.ShapeDtypeStruct((T, D_MODEL), F32),
        scratch_shapes=[
            pltpu.VMEM((tm, D_MODEL), BF16),
            pltpu.VMEM((ns, N_EXPERTS, ts), F32),
            pltpu.VMEM((ns, N_EXPERTS, ts), F32),
            pltpu.SMEM((ns * N_EXPERTS,), jnp.int32),
            pltpu.VMEM((ns, cap, D_MODEL), BF16),
            pltpu.VMEM((ns, cap, D_MODEL), F32),
        ],
        compiler_params=_params(("parallel", "arbitrary", "arbitrary"), 58),
        name="moe",
    )(x, nw, wr_hi, wr_lo, wgu, wd, n_final)


def _pad_lanes(v, width=128):
    v = v.reshape(1, -1).astype(F32)
    return jnp.pad(v, ((0, 0), (0, width - v.shape[1])))


def _prepare(norm_mix, w_in, sgu_ln_gain, sgu_ln_bias, sgu_w_spatial, sgu_b_spatial,
             dn_conv_w, dn_a_log, dn_dt_bias, dn_out_norm, w_out, norm_ffn,
             ffn_w_gate, ffn_w_up, ffn_w_down, moe_w_router, moe_w_gate, moe_w_up,
             moe_w_down, norm_final):
    depth = w_in.shape[0]
    W = D_MODEL
    layers = []
    for i in range(depth):
        wi = w_in[i]
        seg = lambda a, b: wi[:, a:b]
        ab0 = 6 * W
        ab1 = ab0 + 2 * N_DIR * N_HEADS
        w_main = jnp.concatenate(
            [seg(2 * W, 5 * W), seg(0, 2 * W), seg(5 * W, 6 * W), seg(ab1, ab1 + 2 * W)],
            axis=1).astype(BF16)
        w_ab = jnp.pad(seg(ab0, ab1), ((0, 0), (0, 128 - (ab1 - ab0)))).astype(BF16)
        lp = dict(
            norm_mix=norm_mix[i].reshape(1, W), w_main=w_main, w_ab=w_ab,
            ln_g=sgu_ln_gain[i].reshape(1, W), ln_b=sgu_ln_bias[i].reshape(1, W),
            w_sp=sgu_w_spatial[i].astype(BF16), b_spT=jnp.transpose(sgu_b_spatial[i]),
            conv_w=dn_conv_w[i],
            neg_exp_alog=_pad_lanes(-jnp.exp(dn_a_log[i].astype(F32))),
            dt_bias=_pad_lanes(dn_dt_bias[i]),
            o_norm=dn_out_norm[i].reshape(1, HEAD_DIM), w_out=w_out[i].astype(BF16),
            norm_ffn=norm_ffn[i].reshape(1, W),
        )
        j = i // 2
        if i % 2 == 0:
            lp.update(wg=ffn_w_gate[j].astype(BF16), wu=ffn_w_up[j].astype(BF16),
                      wd=ffn_w_down[j].astype(BF16))
        else:
            wrT = jnp.transpose(moe_w_router[j]).astype(F32)
            wr_hi = wrT.astype(BF16)
            tf = MOE_FFN_BLOCK
            wgu = jnp.concatenate(
                [w[:, :, f * tf:(f + 1) * tf].astype(BF16)
                 for f in range(moe_w_gate.shape[-1] // tf) for w in (moe_w_gate[j], moe_w_up[j])],
                axis=-1)
            lp.update(wr_hi=wr_hi, wr_lo=(wrT - wr_hi.astype(F32)).astype(BF16),
                      wgu=wgu, wd=moe_w_down[j].astype(BF16))
        layers.append(lp)
    return layers, norm_final.reshape(1, W)


def _trunk(x3, layers, n_final):
    B, L, W = x3.shape
    x = x3.reshape(B * L, W)
    depth = len(layers)
    assert depth % 2 == 0, "the final RMSNorm is fused into the last (expert) layer"
    for i, lp in enumerate(layers):
        P, ab = _in_proj(x, lp["norm_mix"], lp["w_main"], lp["w_ab"])
        qn, kn, vc, knT, gcol, gT = _dn_prep(P, ab, lp["conv_w"], lp["neg_exp_alog"],
                                             lp["dt_bias"], L)
        o_f, o_b = _dn_scan(qn, kn, vc, knT, gcol, gT, L)
        x = _mix(P, o_f, o_b, x, lp["ln_g"], lp["ln_b"], lp["w_sp"], lp["b_spT"],
                 lp["o_norm"], lp["w_out"])
        if i % 2 == 0:
            x = _ffn(x, lp["norm_ffn"], lp["wg"], lp["wu"], lp["wd"])
        else:
            assert i == depth - 1
            x = _moe(x, lp["norm_ffn"], lp["wr_hi"], lp["wr_lo"], lp["wgu"], lp["wd"], n_final)
    return x.reshape(B, L, W)


def kernel(x_prompt, x_sample, norm_mix, w_in, sgu_ln_gain, sgu_ln_bias, sgu_w_spatial, sgu_b_spatial, dn_conv_w, dn_a_log, dn_dt_bias, dn_out_norm, w_out, norm_ffn, ffn_w_gate, ffn_w_up, ffn_w_down, moe_w_router, moe_w_gate, moe_w_up, moe_w_down, norm_final):
    layers, n_final = _prepare(norm_mix, w_in, sgu_ln_gain, sgu_ln_bias, sgu_w_spatial,
                               sgu_b_spatial, dn_conv_w, dn_a_log, dn_dt_bias, dn_out_norm,
                               w_out, norm_ffn, ffn_w_gate, ffn_w_up, ffn_w_down, moe_w_router,
                               moe_w_gate, moe_w_up, moe_w_down, norm_final)
    return (_trunk(x_prompt, layers, n_final), _trunk(x_sample, layers, n_final))
```

```python
import functools

import jax
import jax.numpy as jnp
from jax import lax
from jax.experimental import pallas as pl
from jax.experimental.pallas import tpu as pltpu

F32 = jnp.float32
BF16 = jnp.bfloat16
EPS = 1e-6

D_MODEL = 1024
N_HEADS = 8
HEAD_DIM = 128
N_DIR = 2
CONV_K = 5
CONV_HALO = 8
HALO_BLOCK = 16
SGU_CHUNK = 128
SGU_GROUPS = 8
N_EXPERTS = 8
SCAN_CHUNK = 128
MOE_UNIT = 96
MOE_SMALL_BLOCK = 2 * MOE_UNIT
MOE_LARGE_BLOCK = 3 * MOE_UNIT
MOE_SUB_TILE = 1024
MOE_SUB_TILES = 2
MOE_FFN_BLOCK = 896
NEG_BIG = -1e30


def _params(sem, vmem_mb):
    return pltpu.CompilerParams(dimension_semantics=sem, vmem_limit_bytes=vmem_mb * 2**20)


def _dot(a, b):
    return jnp.dot(a.astype(BF16), b.astype(BF16), preferred_element_type=F32)


def _rms(xf, w):
    return xf * lax.rsqrt(jnp.mean(xf * xf, axis=-1, keepdims=True) + EPS) * w


def _silu(x):
    return x * jax.nn.sigmoid(x)


def _gelu(x):
    return 0.5 * x * (1.0 + lax.erf(x * (2.0 ** -0.5)))


def _in_proj_kernel(x_ref, nw_ref, w_ref, wab_ref, p_ref, ab_ref):
    hb = _rms(x_ref[...], nw_ref[...]).astype(BF16)
    ab_ref[...] = jnp.dot(hb, wab_ref[...], preferred_element_type=F32)
    for j in range(w_ref.shape[1] // D_MODEL):
        cs = slice(j * D_MODEL, (j + 1) * D_MODEL)
        p_ref[:, cs] = jnp.dot(hb, w_ref[:, cs], preferred_element_type=F32).astype(BF16)


def _in_proj(x, nw, w_main, w_ab):
    T = x.shape[0]
    tm = min(512, T)
    n_cols = w_main.shape[1]
    resident = lambda shape: pl.BlockSpec(shape, lambda i: (0, 0), pipeline_mode=pl.Buffered(1))
    return pl.pallas_call(
        _in_proj_kernel,
        grid=(T // tm,),
        in_specs=[
            pl.BlockSpec((tm, D_MODEL), lambda i: (i, 0)),
            pl.BlockSpec((1, D_MODEL), lambda i: (0, 0)),
            resident((D_MODEL, n_cols)),
            resident((D_MODEL, 128)),
        ],
        out_specs=[
            pl.BlockSpec((tm, n_cols), lambda i: (i, 0)),
            pl.BlockSpec((tm, 128), lambda i: (i, 0)),
        ],
        out_shape=[
            jax.ShapeDtypeStruct((T, n_cols), BF16),
            jax.ShapeDtypeStruct((T, 128), F32),
        ],
        compiler_params=_params(("parallel",), 52),
        name="in_proj",
    )(x, nw, w_main, w_ab)


def _dn_prep_kernel(tiles_per_seq, pm_ref, pp_ref, pn_ref, ab_ref, cw_ref, nea_ref, dtb_ref,
                    qn_ref, kn_ref, vc_ref, knT_ref, gcol_ref, gT_ref, xp_ref):
    tm = pm_ref.shape[0]
    pos = pl.program_id(0) % tiles_per_seq
    keep_prev = jnp.where(pos == 0, 0.0, 1.0)
    keep_next = jnp.where(pos == tiles_per_seq - 1, 0.0, 1.0)
    q_scale = HEAD_DIM ** -0.5
    half = CONV_K // 2
    E = CONV_HALO
    ri = lax.broadcasted_iota(jnp.int32, (tm, tm), 0)
    ci = lax.broadcasted_iota(jnp.int32, (tm, tm), 1)
    offsets = [d for d in range(-half, half + 1) if d != 0]
    shift_mat = jnp.concatenate([jnp.where(ci == ri + d, 1.0, 0.0).astype(BF16) for d in offsets],
                                axis=0)

    for s in range(3):
        cs = slice(s * D_MODEL, (s + 1) * D_MODEL)
        xb = pm_ref[:, cs]
        xf = xb.astype(F32)
        taps = [cw_ref[k:k + 1, cs] for k in range(CONV_K)]
        shifted = jnp.dot(shift_mat, xb, preferred_element_type=F32)
        acc = None
        for k in range(CONV_K):
            d = k - half
            src = xf if d == 0 else shifted[offsets.index(d) * tm:(offsets.index(d) + 1) * tm]
            acc = src * taps[k] if acc is None else acc + src * taps[k]
        xp_ref[0, 0:E, :] = pp_ref[:, cs].astype(F32)[HALO_BLOCK - E:, :] * keep_prev
        xp_ref[0, E:3 * E, :] = xf[0:2 * E]
        xp_ref[1, 0:2 * E, :] = xf[tm - 2 * E:tm]
        xp_ref[1, 2 * E:3 * E, :] = pn_ref[:, cs].astype(F32)[:E, :] * keep_next
        edges = []
        for side in range(2):
            e_acc = None
            for k in range(CONV_K):
                start = E - half + k
                term = xp_ref[side, start:start + E, :] * taps[k]
                e_acc = term if e_acc is None else e_acc + term
            edges.append(e_acc)
        y = _silu(jnp.concatenate([edges[0], acc[E:tm - E], edges[1]], axis=0))
        if s == 2:
            vc_ref[...] = y.astype(BF16)
            continue
        for h in range(N_HEADS):
            hs = slice(h * HEAD_DIM, (h + 1) * HEAD_DIM)
            slab = y[:, hs]
            slab = slab * lax.rsqrt(jnp.sum(slab * slab, axis=-1, keepdims=True) + EPS)
            if s == 0:
                qn_ref[:, hs] = (slab * q_scale).astype(BF16)
            else:
                kn_ref[:, hs] = slab.astype(BF16)
                knT_ref[hs, :] = jnp.transpose(slab).astype(BF16)

    ab = ab_ref[...]
    z = ab + dtb_ref[...]
    softplus = jnp.maximum(z, 0.0) + jnp.log(1.0 + jnp.exp(-jnp.abs(z)))
    g = nea_ref[...] * softplus
    beta = jax.nn.sigmoid(ab)
    ri = lax.broadcasted_iota(jnp.int32, (tm, tm), 0)
    ci = lax.broadcasted_iota(jnp.int32, (tm, tm), 1)
    same = jnp.right_shift(ri, 7) == jnp.right_shift(ci, 7)
    lower = jnp.where(same & (ci <= ri), 1.0, 0.0).astype(BF16)
    upper = jnp.where(same & (ci >= ri), 1.0, 0.0).astype(BF16)
    g_hi = g.astype(BF16)
    g_lo = (g - g_hi.astype(F32)).astype(BF16)
    g_fwd = (jnp.dot(lower, g_hi, preferred_element_type=F32)
             + jnp.dot(lower, g_lo, preferred_element_type=F32))
    g_bwd = (jnp.dot(upper, g_hi, preferred_element_type=F32)
             + jnp.dot(upper, g_lo, preferred_element_type=F32))
    lane = lax.broadcasted_iota(jnp.int32, g.shape, 1)
    gcol = jnp.where(lane < N_HEADS, g_fwd, jnp.where(lane < N_DIR * N_HEADS, g_bwd, beta))
    gcol_ref[...] = gcol
    gT_ref[...] = jnp.transpose(gcol)[0:N_DIR * N_HEADS, :]


def _dn_prep(P, ab, conv_w, neg_exp_alog, dt_bias, seq_len):
    T = P.shape[0]
    tm = min(256, seq_len)
    tiles_per_seq = seq_len // tm
    hb = tm // HALO_BLOCK
    n_hb = T // HALO_BLOCK
    qkv_w = 3 * D_MODEL
    tok = jax.ShapeDtypeStruct((T, D_MODEL), BF16)
    tok_spec = pl.BlockSpec((tm, D_MODEL), lambda i: (i, 0))
    return pl.pallas_call(
        functools.partial(_dn_prep_kernel, tiles_per_seq),
        grid=(T // tm,),
        in_specs=[
            pl.BlockSpec((tm, qkv_w), lambda i: (i, 0)),
            pl.BlockSpec((HALO_BLOCK, qkv_w), lambda i: (jnp.maximum(i * hb - 1, 0), 0)),
            pl.BlockSpec((HALO_BLOCK, qkv_w), lambda i: (jnp.minimum((i + 1) * hb, n_hb - 1), 0)),
            pl.BlockSpec((tm, 128), lambda i: (i, 0)),
            pl.BlockSpec((CONV_K, qkv_w), lambda i: (0, 0)),
            pl.BlockSpec((1, 128), lambda i: (0, 0)),
            pl.BlockSpec((1, 128), lambda i: (0, 0)),
        ],
        out_specs=[
            tok_spec, tok_spec, tok_spec,
            pl.BlockSpec((D_MODEL, tm), lambda i: (0, i)),
            pl.BlockSpec((tm, 128), lambda i: (i, 0)),
            pl.BlockSpec((N_DIR * N_HEADS, tm), lambda i: (0, i)),
        ],
        out_shape=[tok, tok, tok,
                   jax.ShapeDtypeStruct((D_MODEL, T), BF16),
                   jax.ShapeDtypeStruct((T, 128), F32),
                   jax.ShapeDtypeStruct((N_DIR * N_HEADS, T), F32)],
        scratch_shapes=[pltpu.VMEM((2, 3 * CONV_HALO, D_MODEL), F32)],
        compiler_params=_params(("parallel",), 48),
        name="dn_prep",
    )(P, P, P, ab, conv_w, neg_exp_alog, dt_bias)


def _stack(a, b):
    return jnp.concatenate([a.astype(BF16), b.astype(BF16)], axis=0)


def _merge_rows(direction, blk):
    first = blk if direction == 0 else 0
    return [slice(s, s + blk) for s in range(first, SCAN_CHUNK, 2 * blk)]


def _dn_scan_kernel(qf_ref, kf_ref, vf_ref, kTf_ref, gcf_ref, gTf_ref,
                    qb_ref, kb_ref, vb_ref, kTb_ref, gcb_ref, gTb_ref,
                    of_ref, ob_ref, S_ref):
    CH = SCAN_CHUNK
    NJ = qf_ref.shape[0] // CH
    refs = ((qf_ref, kf_ref, vf_ref, kTf_ref, gcf_ref, gTf_ref, of_ref),
            (qb_ref, kb_ref, vb_ref, kTb_ref, gcb_ref, gTb_ref, ob_ref))
    chains = [(d, j, h) for d in range(N_DIR) for j in range(NJ) for h in range(N_HEADS)]
    C = range(len(chains))
    tok = lambda j: slice(j * CH, (j + 1) * CH)

    @pl.when(pl.program_id(1) == 0)
    def _():
        S_ref[...] = jnp.zeros_like(S_ref)

    ri = lax.broadcasted_iota(jnp.int32, (CH, CH), 0)
    ci = lax.broadcasted_iota(jnp.int32, (CH, CH), 1)
    incl = (ri >= ci, ri <= ci)
    strict = (ri > ci, ri < ci)
    end = (CH - 1, 0)
    eye = jnp.where(ri == ci, 1.0, 0.0)
    same = lambda sh: jnp.right_shift(ri, sh) == jnp.right_shift(ci, sh)
    m16, m32, m64 = same(4), same(5), same(6)
    levels = ((16, m32 & ~m16), (32, m64 & ~m32), (64, ~m64))
    hs = lambda h: slice(h * HEAD_DIM, (h + 1) * HEAD_DIM)
    gc = [r[4][...] for r in refs]
    gT = [r[5][...] for r in refs]

    def col(d, j, h, base):
        l = base + d * N_HEADS + h
        return gc[d][tok(j), l:l + 1]

    Gc = [col(d, j, h, 0) for d, j, h in chains]
    beta = [col(d, j, h, N_DIR * N_HEADS) for d, j, h in chains]
    Gr = [gT[d][d * N_HEADS + h:d * N_HEADS + h + 1, tok(j)] for d, j, h in chains]
    G_end = [Gr[c][:, end[chains[c][0]]:end[chains[c][0]] + 1] for c in C]
    q = [refs[d][0][tok(j), hs(h)] for d, j, h in chains]
    k = [refs[d][1][tok(j), hs(h)] for d, j, h in chains]
    kT = [refs[d][3][hs(h), tok(j)] for d, j, h in chains]

    decay = [jnp.exp(jnp.where(incl[chains[c][0]], Gc[c] - Gr[c], NEG_BIG)) for c in C]
    KQ = [jnp.dot(jnp.concatenate([k[c], q[c]], axis=0), kT[c], preferred_element_type=F32) for c in C]
    A = [jnp.where(strict[chains[c][0]], beta[c] * KQ[c][:CH] * decay[c], 0.0) for c in C]
    qkm = [(KQ[c][CH:] * decay[c]).astype(BF16) for c in C]
    Dg = [jnp.where(m16, A[c], 0.0) for c in C]
    D2 = [_dot(Dg[c], Dg[c]) for c in C]
    X = [eye - Dg[c] for c in C]
    Y = [_dot(_stack(D2[c], X[c]), D2[c]) for c in C]
    D4 = [Y[c][:CH] for c in C]
    X = [X[c] + Y[c][CH:] for c in C]
    Y = [_dot(_stack(D4[c], X[c]), D4[c]) for c in C]
    X = [X[c] + Y[c][CH:] for c in C]
    X = [X[c] + _dot(X[c], Y[c][:CH]) for c in C]
    for blk, off in levels:
        rows = [_merge_rows(chains[c][0], blk) for c in C]
        Xr = [jnp.concatenate([X[c][r] for r in rows[c]], axis=0) for c in C]
        Z = [_dot(Xr[c], jnp.where(off, A[c], 0.0)) for c in C]
        Xr = [Xr[c] - _dot(Z[c], X[c]) for c in C]
        Xn = []
        for c in C:
            parts, pos = [], 0
            for j, r in enumerate(rows[c]):
                if r.start > pos:
                    parts.append(X[c][pos:r.start])
                parts.append(Xr[c][j * blk:(j + 1) * blk])
                pos = r.stop
            if pos < CH:
                parts.append(X[c][pos:CH])
            Xn.append(jnp.concatenate(parts, axis=0))
        X = Xn

    eG = [jnp.exp(Gc[c]) for c in C]
    kbe = [(k[c].astype(F32) * (beta[c] * eG[c])).astype(BF16) for c in C]
    vbeta = [refs[d][2][tok(j), hs(h)].astype(F32) * beta[c] for c, (d, j, h) in enumerate(chains)]
    qg = [(q[c].astype(F32) * eG[c]).astype(BF16) for c in C]
    keT = [(kT[c].astype(F32) * jnp.exp(G_end[c] - Gr[c])).astype(BF16) for c in C]
    dec = [jnp.exp(G_end[c]) for c in C]

    heads = [(d, h) for d in range(N_DIR) for h in range(N_HEADS)]
    Hd = range(len(heads))
    S = [S_ref[i] for i in Hd]
    for t in range(NJ):
        cur = [chains.index((d, t if d == 0 else NJ - 1 - t, h)) for d, h in heads]
        Sb = [S[i].astype(BF16) for i in Hd]
        kS = [jnp.dot(kbe[cur[i]], Sb[i], preferred_element_type=F32) for i in Hd]
        vnb = [_dot(X[cur[i]], vbeta[cur[i]] - kS[i]).astype(BF16) for i in Hd]
        o = [jnp.dot(jnp.concatenate([qg[cur[i]], qkm[cur[i]]], axis=1),
                     jnp.concatenate([Sb[i], vnb[i]], axis=0), preferred_element_type=F32) for i in Hd]
        dS = [jnp.dot(keT[cur[i]], vnb[i], preferred_element_type=F32) for i in Hd]
        for i in Hd:
            d, j, h = chains[cur[i]]
            refs[d][6][tok(j), hs(h)] = o[i].astype(refs[d][6].dtype)
        S = [S[i] * dec[cur[i]] + dS[i] for i in Hd]
    for i in Hd:
        S_ref[i] = S[i]


SCAN_BLOCK = 2 * SCAN_CHUNK


def _dn_scan(qn, kn, vc, knT, gcol, gT, seq_len):
    T = qn.shape[0]
    CH = min(SCAN_BLOCK, seq_len)
    nc = seq_len // CH
    nb = T // seq_len
    fwd = lambda b, c: b * nc + c
    bwd = lambda b, c: b * nc + (nc - 1 - c)

    def specs(blk):
        tok = pl.BlockSpec((CH, D_MODEL), lambda b, c: (blk(b, c), 0))
        return [tok, tok, tok,
                pl.BlockSpec((D_MODEL, CH), lambda b, c: (0, blk(b, c))),
                pl.BlockSpec((CH, 128), lambda b, c: (blk(b, c), 0)),
                pl.BlockSpec((N_DIR * N_HEADS, CH), lambda b, c: (0, blk(b, c)))]

    out = jax.ShapeDtypeStruct((T, D_MODEL), BF16)
    args = (qn, kn, vc, knT, gcol, gT)
    return pl.pallas_call(
        _dn_scan_kernel,
        grid=(nb, nc),
        in_specs=specs(fwd) + specs(bwd),
        out_specs=[specs(fwd)[0], specs(bwd)[0]],
        out_shape=[out, out],
        scratch_shapes=[pltpu.VMEM((N_DIR * N_HEADS, HEAD_DIM, HEAD_DIM), F32)],
        compiler_params=_params(("parallel", "arbitrary"), 32),
        name="dn_scan",
    )(*args, *args)


def _mix_kernel(u_ref, vs_ref, z_ref, ga_ref, gb_ref, of_ref, ob_ref, x_ref,
                lng_ref, lnb_ref, ws_ref, bs_ref, on_ref, wo_ref, out_ref, mix_ref):
    tm = x_ref.shape[0]
    v = _gelu(vs_ref[...].astype(F32))
    vcen = v - jnp.mean(v, axis=-1, keepdims=True)
    vln = vcen * lax.rsqrt(jnp.mean(vcen * vcen, axis=-1, keepdims=True) + EPS)
    vb = (vln * lng_ref[...] + lnb_ref[...]).astype(BF16)
    for g in range(SGU_GROUPS):
        gs = slice(g * HEAD_DIM, (g + 1) * HEAD_DIM)
        u = _gelu(u_ref[:, gs].astype(F32))
        gate_a = jax.nn.sigmoid(ga_ref[:, gs].astype(F32))
        o = of_ref[:, gs].astype(F32) + ob_ref[:, gs].astype(F32)
        o = o * lax.rsqrt(jnp.mean(o * o, axis=-1, keepdims=True) + EPS) * on_ref[...]
        o = o * _silu(z_ref[:, gs].astype(F32))
        mixed = jax.nn.sigmoid(gb_ref[:, gs].astype(F32)) * o
        w_sp = ws_ref[g]
        bias = bs_ref[:, g:g + 1]
        for c in range(tm // SGU_CHUNK):
            rs = slice(c * SGU_CHUNK, (c + 1) * SGU_CHUNK)
            s = jnp.dot(w_sp, vb[rs, gs], preferred_element_type=F32) + bias
            mix_ref[rs, gs] = (mixed[rs] + gate_a[rs] * (u[rs] * s)).astype(BF16)
    out_ref[...] = x_ref[...] + jnp.dot(mix_ref[...], wo_ref[...], preferred_element_type=F32)


def _mix(P, o_f, o_b, x, ln_g, ln_b, w_sp, b_spT, o_norm, w_out):
    T = x.shape[0]
    tm = min(256, T)
    col = lambda j: pl.BlockSpec((tm, D_MODEL), lambda i: (i, j))
    tok = pl.BlockSpec((tm, D_MODEL), lambda i: (i, 0))
    vec = pl.BlockSpec((1, D_MODEL), lambda i: (0, 0))
    return pl.pallas_call(
        _mix_kernel,
        grid=(T // tm,),
        in_specs=[
            col(3), col(4), col(5), col(6), col(7), tok, tok, tok, vec, vec,
            pl.BlockSpec((SGU_GROUPS, SGU_CHUNK, SGU_CHUNK), lambda i: (0, 0, 0)),
            pl.BlockSpec((SGU_CHUNK, SGU_GROUPS), lambda i: (0, 0)),
            pl.BlockSpec((1, HEAD_DIM), lambda i: (0, 0)),
            pl.BlockSpec((D_MODEL, D_MODEL), lambda i: (0, 0)),
        ],
        out_specs=tok,
        out_shape=jax.ShapeDtypeStruct((T, D_MODEL), F32),
        scratch_shapes=[pltpu.VMEM((tm, D_MODEL), BF16)],
        compiler_params=_params(("parallel",), 48),
        name="mix",
    )(P, P, P, P, P, o_f, o_b, x, ln_g, ln_b, w_sp, b_spT, o_norm, w_out)


FFN_BLOCK = 256


def _ffn_kernel(x_ref, nw_ref, wg_ref, wu_ref, wd_ref, o_ref):
    xf = x_ref[...]
    h = _rms(xf, nw_ref[...]).astype(BF16)
    o_ref[...] = xf
    for f in range(wg_ref.shape[1] // FFN_BLOCK):
        fs = slice(f * FFN_BLOCK, (f + 1) * FFN_BLOCK)
        g = jnp.dot(h, wg_ref[:, fs], preferred_element_type=F32)
        u = jnp.dot(h, wu_ref[:, fs], preferred_element_type=F32)
        o_ref[...] += jnp.dot((_silu(g) * u).astype(BF16), wd_ref[fs, :], preferred_element_type=F32)


def _ffn(x, nw, wg, wu, wd):
    T = x.shape[0]
    F = wg.shape[1]
    tm = min(512, T)
    tok = pl.BlockSpec((tm, D_MODEL), lambda i: (i, 0))
    resident = lambda shape: pl.BlockSpec(shape, lambda i: (0, 0), pipeline_mode=pl.Buffered(1))
    return pl.pallas_call(
        _ffn_kernel,
        grid=(T // tm,),
        in_specs=[
            tok,
            pl.BlockSpec((1, D_MODEL), lambda i: (0, 0)),
            resident((D_MODEL, F)), resident((D_MODEL, F)), resident((F, D_MODEL)),
        ],
        out_specs=tok,
        out_shape=jax.ShapeDtypeStruct((T, D_MODEL), F32),
        compiler_params=_params(("parallel",), 52),
        name="ffn",
    )(x, nw, wg, wu, wd)


def _moe_kernel(nf, ns, x_ref, nw_ref, wrh_ref, wrl_ref, wgu_ref, wd_ref, nfin_ref, o_ref,
                h_ref, rank_ref, gate_ref, cnt_ref, xe_ref, ye_ref):
    e = pl.program_id(1)
    f = pl.program_id(2)
    Ts = x_ref.shape[0] // ns
    nt = (((1,), (1,)), ((), ()))
    tn = (((0,), (0,)), ((), ()))

    @pl.when((e == 0) & (f == 0))
    def _route():
        ri = lax.broadcasted_iota(jnp.int32, (Ts, Ts), 0)
        ci = lax.broadcasted_iota(jnp.int32, (Ts, Ts), 1)
        before = jnp.where(ri < ci, 1.0, 0.0).astype(BF16)
        wrh = wrh_ref[...]
        for s in range(ns):
            tok = slice(s * Ts, (s + 1) * Ts)
            xf = x_ref[tok, :]
            h = _rms(xf, nw_ref[...])
            hb = h.astype(BF16)
            h_ref[tok, :] = hb
            hl = (h - hb.astype(F32)).astype(BF16)
            logits = (lax.dot_general(wrh, hb, nt, preferred_element_type=F32)
                      + lax.dot_general(wrh, hl, nt, preferred_element_type=F32)
                      + lax.dot_general(wrl_ref[...], hb, nt, preferred_element_type=F32))
            ei = lax.broadcasted_iota(jnp.int32, logits.shape, 0)
            m1 = jnp.max(logits, axis=0, keepdims=True)
            i1 = jnp.min(jnp.where(logits == m1, ei, N_EXPERTS), axis=0, keepdims=True)
            s1 = ei == i1
            rest = jnp.where(s1, -jnp.inf, logits)
            m2 = jnp.max(rest, axis=0, keepdims=True)
            i2 = jnp.min(jnp.where(rest == m2, ei, N_EXPERTS), axis=0, keepdims=True)
            s2 = ei == i2
            e2 = jnp.exp(m2 - m1)
            w1 = 1.0 / (1.0 + e2)
            gate_ref[s] = jnp.where(s1, w1, 0.0) + jnp.where(s2, e2 * w1, 0.0)
            sel = jnp.where(s1 | s2, 1.0, 0.0)
            rank = jnp.dot(sel.astype(BF16), before, preferred_element_type=F32)
            rank_ref[s] = jnp.where(sel > 0.0, rank, -1.0)
            cnt = jnp.sum(sel, axis=1, keepdims=True)
            for ee in range(N_EXPERTS):
                cnt_ref[s * N_EXPERTS + ee] = cnt[ee, 0].astype(jnp.int32)
            o_ref[tok, :] = xf

    for s in range(ns):
        tok = slice(s * Ts, (s + 1) * Ts)
        n_unit = (cnt_ref[s * N_EXPERTS + e] + MOE_UNIT - 1) // MOE_UNIT
        rem = n_unit % 3
        n_small = jnp.where(n_unit == 1, 1, jnp.where(rem == 0, 0, jnp.where(rem == 2, 1, 2)))
        n_large = (jnp.maximum(n_unit, 2) - 2 * n_small) // 3
        large_base = n_small * MOE_SMALL_BLOCK
        rk = rank_ref[s, pl.ds(e, 1), :]

        def for_blocks(fn, n_small=n_small, n_large=n_large, large_base=large_base):
            def small_body(j, carry):
                fn(pl.multiple_of(j * MOE_SMALL_BLOCK, MOE_SMALL_BLOCK), MOE_SMALL_BLOCK)
                return carry
            lax.fori_loop(0, n_small, small_body, 0)

            def large_body(j, carry):
                fn(pl.multiple_of(large_base + j * MOE_LARGE_BLOCK, MOE_UNIT), MOE_LARGE_BLOCK)
                return carry
            lax.fori_loop(0, n_large, large_body, 0)

        def one_hot(first, size, value, rk=rk):
            rows = (lax.broadcasted_iota(jnp.int32, (size, Ts), 0) + first).astype(F32)
            return jnp.where(rk == rows, value, 0.0).astype(BF16)

        @pl.when(f == 0)
        def _gather(s=s, tok=tok, for_blocks=for_blocks, one_hot=one_hot):
            def gather_block(first, size):
                xe_ref[s, pl.ds(first, size), :] = jnp.dot(
                    one_hot(first, size, 1.0), h_ref[tok, :], preferred_element_type=F32).astype(BF16)
                ye_ref[s, pl.ds(first, size), :] = jnp.zeros((size, D_MODEL), F32)
            for_blocks(gather_block)

        def expert_block(first, size, s=s):
            rows = pl.ds(first, size)
            xb = xe_ref[s, rows, :]
            gu = jnp.dot(xb, wgu_ref[0], preferred_element_type=F32)
            g, u = gu[:, :MOE_FFN_BLOCK], gu[:, MOE_FFN_BLOCK:]
            ye_ref[s, rows, :] += jnp.dot((_silu(g) * u).astype(BF16), wd_ref[0],
                                          preferred_element_type=F32)
        for_blocks(expert_block)

        @pl.when(f == nf - 1)
        def _scatter(s=s, tok=tok, for_blocks=for_blocks, one_hot=one_hot):
            gt = gate_ref[s, pl.ds(e, 1), :]

            def scatter_block(first, size):
                o_ref[tok, :] += lax.dot_general(
                    one_hot(first, size, gt), ye_ref[s, pl.ds(first, size), :].astype(BF16),
                    tn, preferred_element_type=F32)
            for_blocks(scatter_block)

    @pl.when((e == N_EXPERTS - 1) & (f == nf - 1))
    def _final():
        o_ref[...] = _rms(o_ref[...], nfin_ref[...])


def _moe(x, nw, wr_hi, wr_lo, wgu, wd, n_final):
    T = x.shape[0]
    tf = MOE_FFN_BLOCK
    nf = wd.shape[1] // tf
    ts = min(MOE_SUB_TILE, T)
    ns = min(MOE_SUB_TILES, T // ts)
    tm = ns * ts
    cap = max(-(-ts // MOE_UNIT) * MOE_UNIT, MOE_SMALL_BLOCK)
    tok = pl.BlockSpec((tm, D_MODEL), lambda i, e, f: (i, 0), pipeline_mode=pl.Buffered(1))
    vec = pl.BlockSpec((1, D_MODEL), lambda i, e, f: (0, 0))
    wr = pl.BlockSpec((N_EXPERTS, D_MODEL), lambda i, e, f: (0, 0))
    w_in = pl.BlockSpec((1, D_MODEL, 2 * tf), lambda i, e, f: (e, 0, f))
    return pl.pallas_call(
        functools.partial(_moe_kernel, nf, ns),
        grid=(T // tm, N_EXPERTS, nf),
        in_specs=[
            pl.BlockSpec((tm, D_MODEL), lambda i, e, f: (i, 0)), vec, wr, wr, w_in,
            pl.BlockSpec((1, tf, D_MODEL), lambda i, e, f: (e, f, 0)),
            vec,
        ],
        out_specs=tok,
        out_shape=jax.ShapeDtypeStruct((T, D_MODEL), F32),
        scratch_shapes=[
            pltpu.VMEM((tm, D_MODEL), BF16),
            pltpu.VMEM((ns, N_EXPERTS, ts), F32),
            pltpu.VMEM((ns, N_EXPERTS, ts), F32),
            pltpu.SMEM((ns * N_EXPERTS,), jnp.int32),
            pltpu.VMEM((ns, cap, D_MODEL), BF16),
            pltpu.VMEM((ns, cap, D_MODEL), F32),
        ],
        compiler_params=_params(("parallel", "arbitrary", "arbitrary"), 58),
        name="moe",
    )(x, nw, wr_hi, wr_lo, wgu, wd, n_final)


def _pad_lanes(v, width=128):
    v = v.reshape(1, -1).astype(F32)
    return jnp.pad(v, ((0, 0), (0, width - v.shape[1])))


def _prepare(norm_mix, w_in, sgu_ln_gain, sgu_ln_bias, sgu_w_spatial, sgu_b_spatial,
             dn_conv_w, dn_a_log, dn_dt_bias, dn_out_norm, w_out, norm_ffn,
             ffn_w_gate, ffn_w_up, ffn_w_down, moe_w_router, moe_w_gate, moe_w_up,
             moe_w_down, norm_final):
    depth = w_in.shape[0]
    W = D_MODEL
    layers = []
    for i in range(depth):
        wi = w_in[i]
        seg = lambda a, b: wi[:, a:b]
        ab0 = 6 * W
        ab1 = ab0 + 2 * N_DIR * N_HEADS
        w_main = jnp.concatenate(
            [seg(2 * W, 5 * W), seg(0, 2 * W), seg(5 * W, 6 * W), seg(ab1, ab1 + 2 * W)],
            axis=1).astype(BF16)
        w_ab = jnp.pad(seg(ab0, ab1), ((0, 0), (0, 128 - (ab1 - ab0)))).astype(BF16)
        lp = dict(
            norm_mix=norm_mix[i].reshape(1, W), w_main=w_main, w_ab=w_ab,
            ln_g=sgu_ln_gain[i].reshape(1, W), ln_b=sgu_ln_bias[i].reshape(1, W),
            w_sp=sgu_w_spatial[i].astype(BF16), b_spT=jnp.transpose(sgu_b_spatial[i]),
            conv_w=dn_conv_w[i],
            neg_exp_alog=_pad_lanes(-jnp.exp(dn_a_log[i].astype(F32))),
            dt_bias=_pad_lanes(dn_dt_bias[i]),
            o_norm=dn_out_norm[i].reshape(1, HEAD_DIM), w_out=w_out[i].astype(BF16),
            norm_ffn=norm_ffn[i].reshape(1, W),
        )
        j = i // 2
        if i % 2 == 0:
            lp.update(wg=ffn_w_gate[j].astype(BF16), wu=ffn_w_up[j].astype(BF16),
                      wd=ffn_w_down[j].astype(BF16))
        else:
            wrT = jnp.transpose(moe_w_router[j]).astype(F32)
            wr_hi = wrT.astype(BF16)
            tf = MOE_FFN_BLOCK
            wgu = jnp.concatenate(
                [w[:, :, f * tf:(f + 1) * tf].astype(BF16)
                 for f in range(moe_w_gate.shape[-1] // tf) for w in (moe_w_gate[j], moe_w_up[j])],
                axis=-1)
            lp.update(wr_hi=wr_hi, wr_lo=(wrT - wr_hi.astype(F32)).astype(BF16),
                      wgu=wgu, wd=moe_w_down[j].astype(BF16))
        layers.append(lp)
    return layers, norm_final.reshape(1, W)


def _trunk(x3, layers, n_final):
    B, L, W = x3.shape
    x = x3.reshape(B * L, W)
    depth = len(layers)
    assert depth % 2 == 0, "the final RMSNorm is fused into the last (expert) layer"
    for i, lp in enumerate(layers):
        P, ab = _in_proj(x, lp["norm_mix"], lp["w_main"], lp["w_ab"])
        qn, kn, vc, knT, gcol, gT = _dn_prep(P, ab, lp["conv_w"], lp["neg_exp_alog"],
                                             lp["dt_bias"], L)
        o_f, o_b = _dn_scan(qn, kn, vc, knT, gcol, gT, L)
        x = _mix(P, o_f, o_b, x, lp["ln_g"], lp["ln_b"], lp["w_sp"], lp["b_spT"],
                 lp["o_norm"], lp["w_out"])
        if i % 2 == 0:
            x = _ffn(x, lp["norm_ffn"], lp["wg"], lp["wu"], lp["wd"])
        else:
            assert i == depth - 1
            x = _moe(x, lp["norm_ffn"], lp["wr_hi"], lp["wr_lo"], lp["wgu"], lp["wd"], n_final)
    return x.reshape(B, L, W)


def kernel(x_prompt, x_sample, norm_mix, w_in, sgu_ln_gain, sgu_ln_bias, sgu_w_spatial, sgu_b_spatial, dn_conv_w, dn_a_log, dn_dt_bias, dn_out_norm, w_out, norm_ffn, ffn_w_gate, ffn_w_up, ffn_w_down, moe_w_router, moe_w_gate, moe_w_up, moe_w_down, norm_final):
    layers, n_final = _prepare(norm_mix, w_in, sgu_ln_gain, sgu_ln_bias, sgu_w_spatial,
                               sgu_b_spatial, dn_conv_w, dn_a_log, dn_dt_bias, dn_out_norm,
                               w_out, norm_ffn, ffn_w_gate, ffn_w_up, ffn_w_down, moe_w_router,
                               moe_w_gate, moe_w_up, moe_w_down, norm_final)
    return (_trunk(x_prompt, layers, n_final), _trunk(x_sample, layers, n_final))
```
